```python
import functools
import jax, jax.numpy as jnp
from jax import lax
import numpy as np

D_MODEL = 1024
BATCH = 8
SEQ = 2048
DEPTH = 4
DEC_BATCH = 128
DEC_SEQ = 4
PAST_LEN = 2048
PAGE_SIZE = 128

HEAD_DIM = 64
N_MIX_HEADS = D_MODEL // HEAD_DIM
SC_GROUPS = N_MIX_HEADS // 4
RW_HEADS = (N_MIX_HEADS - SC_GROUPS) // 2
SB_HEADS = N_MIX_HEADS - SC_GROUPS - RW_HEADS
SC_WIDTH = SC_GROUPS * HEAD_DIM
RW_WIDTH = RW_HEADS * HEAD_DIM
SB_WIDTH = SB_HEADS * HEAD_DIM
CONV_WIDTH = 3
D_DECAY_LORA = 64
D_A_LORA = 64
D_GATE_LORA = 128
SC_PROJ = 3 * SC_WIDTH
RW_PROJ = 3 * RW_WIDTH + D_DECAY_LORA + D_A_LORA + D_GATE_LORA
SB_PROJ = 3 * SB_WIDTH
IN_PROJ = SC_PROJ + RW_PROJ + SB_PROJ
RW_SPLITS = (RW_WIDTH, 2 * RW_WIDTH, 3 * RW_WIDTH, 3 * RW_WIDTH + D_DECAY_LORA,
             3 * RW_WIDTH + D_DECAY_LORA + D_A_LORA)
D_FF = 4 * D_MODEL
Q_BLOCK = 128
NORM_EPS = 1e-6
RW_GN_EPS = HEAD_DIM * 1e-5
SB_SCALE = HEAD_DIM ** -0.5
SB_BIAS_INIT = -7.0

kernel_name = 'hybrid_shortconv_rwkv7_stickbreak_decoder_step'


def rms_norm(x, g):
    xf = x.astype(jnp.float32)
    y = xf * lax.rsqrt(jnp.mean(xf * xf, axis=-1, keepdims=True) + NORM_EPS)
    return (y * g.astype(jnp.float32)).astype(x.dtype)


def short_conv_mixer(p_sc, conv_buf, conv_w):
    b, c, h = jnp.split(p_sc, 3, axis=-1)
    u = c * h
    T = u.shape[1]
    pad = jnp.concatenate([conv_buf.astype(u.dtype), u], axis=1)
    y = sum(conv_w[i] * pad[:, i:i + T] for i in range(CONV_WIDTH))
    return b * y, pad[:, T:]


def rwkv7_mixer(p_rw, shift_buf, state0, mu_shift, w0, w2, a0, a2, g2, k_k, k_a, r_k, ln_w, ln_b):
    Bsz, T, _ = p_rw.shape
    f32 = jnp.float32
    p = p_rw.astype(f32)
    prev = jnp.concatenate([shift_buf.astype(f32)[:, None], p[:, :-1]], axis=1)
    xs = p + (prev - p) * mu_shift.astype(f32)
    r, k, v, xw, xa, xg = jnp.split(xs, RW_SPLITS, axis=-1)
    log_w = -jax.nn.softplus(-(w0.astype(f32) + jnp.tanh(xw) @ w2.astype(f32))) - 0.5
    decay = jnp.exp(-jnp.exp(log_w))
    a = jax.nn.sigmoid(a0.astype(f32) + xa @ a2.astype(f32))
    g = jax.nn.sigmoid(xg) @ g2.astype(f32)
    kk = k * k_k.astype(f32)
    k = k * (1.0 + (a - 1.0) * k_a.astype(f32))
    heads = lambda t: t.reshape(Bsz, T, RW_HEADS, HEAD_DIM)
    r, k, v, kk, a, decay = map(heads, (r, k, v, kk, a, decay))
    kk = kk / jnp.maximum(jnp.sqrt(jnp.sum(kk * kk, axis=-1, keepdims=True)), 1e-12)

    def step(S, inp):
        r_t, dec_t, k_t, v_t, kk_t, a_t = inp
        sa = jnp.einsum('bhvk,bhk->bhv', S, -kk_t)
        S = (S * dec_t[:, :, None, :] + sa[..., None] * (kk_t * a_t)[:, :, None, :]
             + v_t[..., None] * k_t[:, :, None, :])
        return S, jnp.einsum('bhvk,bhk->bhv', S, r_t)

    seq = tuple(jnp.swapaxes(t, 0, 1) for t in (r, decay, k, v, kk, a))
    S_T, o = lax.scan(step, state0.astype(f32), seq)
    o = jnp.swapaxes(o, 0, 1)
    mean = jnp.mean(o, axis=-1, keepdims=True)
    var = jnp.mean(jnp.square(o - mean), axis=-1, keepdims=True)
    o = ((o - mean) * lax.rsqrt(var + RW_GN_EPS)).reshape(Bsz, T, RW_WIDTH) * ln_w.astype(f32) + ln_b.astype(f32)
    bonus = jnp.sum(r * k * r_k.astype(f32), axis=-1, keepdims=True) * v
    o = (o + bonus.reshape(Bsz, T, RW_WIDTH)) * g
    return o.astype(p_rw.dtype), p_rw[:, -1], S_T


def stick_breaking(q, k, v, bias, q_offset):
    Tq, Tk = q.shape[1], k.shape[1]
    z = (jnp.einsum('bqhd,bkhd->bhqk', q.astype(jnp.float32), k.astype(jnp.float32)) * SB_SCALE
         + bias.astype(jnp.float32)[None, :, None, None])
    mask = jnp.arange(Tk)[None, :] < (q_offset + jnp.arange(Tq))[:, None]
    log_1m = jnp.where(mask, jax.nn.log_sigmoid(-z), 0.0)
    suffix = lax.cumsum(log_1m, axis=3, reverse=True) - log_1m
    A = jnp.where(mask, jnp.exp(jax.nn.log_sigmoid(z) + suffix), 0.0)
    o = jnp.einsum('bhqk,bkhd->bqhd', A, v.astype(jnp.float32))
    return o.astype(q.dtype)


def sb_prompt(q, k, v, bias):
    T = q.shape[1]
    outs = []
    for start in range(0, T, Q_BLOCK):
        end = min(start + Q_BLOCK, T)
        outs.append(stick_breaking(q[:, start:end], k[:, :end], v[:, :end], bias, start))
    return jnp.concatenate(outs, axis=1)


def sb_with_past(q, k, v, bias, past_k, past_v):
    k_all = jnp.concatenate([past_k.astype(k.dtype), k], axis=1)
    v_all = jnp.concatenate([past_v.astype(v.dtype), v], axis=1)
    return stick_breaking(q, k_all, v_all, bias, past_k.shape[1])


def decoder_layer(x, conv_buf, shift_buf, rw_state, attend, w_in, conv_w, mu_shift, w0, w2, a0, a2, g2,
                  k_k, k_a, r_k, ln_w, ln_b, sb_bias, w_out, w_up, w_down,
                  g_pre_mix, g_post_mix, g_pre_ffn, g_post_ffn):
    Bsz, T, _ = x.shape
    h = rms_norm(x, g_pre_mix)
    p = h @ w_in
    p_sc, p_rw, p_sb = jnp.split(p, [SC_PROJ, SC_PROJ + RW_PROJ], axis=-1)
    y_sc, new_conv = short_conv_mixer(p_sc, conv_buf, conv_w)
    y_rw, new_shift, new_state = rwkv7_mixer(p_rw, shift_buf, rw_state, mu_shift, w0, w2, a0, a2, g2,
                                             k_k, k_a, r_k, ln_w, ln_b)
    q, k, v = (t.reshape(Bsz, T, SB_HEADS, HEAD_DIM) for t in jnp.split(p_sb, 3, axis=-1))
    y_sb = attend(q, k, v, sb_bias).reshape(Bsz, T, SB_WIDTH)
    mix = jnp.concatenate([y_sc, y_rw, y_sb], axis=-1) @ w_out
    x = x + rms_norm(mix, g_post_mix)
    f = rms_norm(x, g_pre_ffn)
    f = jnp.square(jax.nn.relu(f @ w_up)) @ w_down
    x = x + rms_norm(f, g_post_ffn)
    return x, new_conv, new_shift, new_state, k, v


def setup_inputs(seed: int = 0) -> dict:
    key = jax.random.key(seed)
    ks = iter(jax.random.split(key, 40))
    f32 = jnp.float32

    def normal(shape, scale):
        return jax.random.normal(next(ks), shape, f32) * scale

    n_pages = PAST_LEN // PAGE_SIZE
    n_pool = (DEC_BATCH * n_pages * 5) // 4
    x_prompt = normal((BATCH, SEQ, D_MODEL), 1.0)
    x_sample = normal((DEC_BATCH, DEC_SEQ, D_MODEL), 1.0)
    state_conv = normal((DEPTH, DEC_BATCH, CONV_WIDTH - 1, SC_WIDTH), 1.0)
    state_shift = normal((DEPTH, DEC_BATCH, RW_PROJ), 1.0)
    state_rwkv = normal((DEPTH, DEC_BATCH, RW_HEADS, HEAD_DIM, HEAD_DIM), 0.5)
    cache_k = normal((DEPTH, n_pool, PAGE_SIZE, SB_HEADS, HEAD_DIM), 1.0)
    cache_v = normal((DEPTH, n_pool, PAGE_SIZE, SB_HEADS, HEAD_DIM), 1.0)
    perm = jax.random.permutation(next(ks), n_pool)
    page_table = perm[:DEC_BATCH * n_pages].reshape(DEC_BATCH, n_pages).astype(jnp.int32)
    return {
        'x_prompt': x_prompt, 'x_sample': x_sample,
        'state_conv': state_conv, 'state_shift': state_shift, 'state_rwkv': state_rwkv,
        'cache_k': cache_k, 'cache_v': cache_v, 'page_table': page_table,
        'w_in': normal((DEPTH, D_MODEL, IN_PROJ), D_MODEL ** -0.5),
        'conv_w': normal((DEPTH, CONV_WIDTH, SC_WIDTH), CONV_WIDTH ** -0.5),
        'mu_shift': jax.random.uniform(next(ks), (DEPTH, RW_PROJ), f32),
        'w0': -2.0 + normal((DEPTH, RW_WIDTH), 0.5),
        'w2': normal((DEPTH, D_DECAY_LORA, RW_WIDTH), 0.5 * D_DECAY_LORA ** -0.5),
        'a0': normal((DEPTH, RW_WIDTH), 0.1),
        'a2': normal((DEPTH, D_A_LORA, RW_WIDTH), 0.5 * D_A_LORA ** -0.5),
        'g2': normal((DEPTH, D_GATE_LORA, RW_WIDTH), D_GATE_LORA ** -0.5),
        'k_k': 0.85 + normal((DEPTH, RW_WIDTH), 0.02),
        'k_a': 1.0 + normal((DEPTH, RW_WIDTH), 0.02),
        'r_k': normal((DEPTH, RW_HEADS, HEAD_DIM), 0.1),
        'ln_w': 1.0 + normal((DEPTH, RW_WIDTH), 0.02),
        'ln_b': normal((DEPTH, RW_WIDTH), 0.02),
        'sb_bias': SB_BIAS_INIT + normal((DEPTH, SB_HEADS), 0.1),
        'w_out': normal((DEPTH, D_MODEL, D_MODEL), D_MODEL ** -0.5),
        'w_up': normal((DEPTH, D_MODEL, D_FF), D_MODEL ** -0.5),
        'w_down': normal((DEPTH, D_FF, D_MODEL), D_FF ** -0.5),
        'g_pre_mix': 1.0 + normal((DEPTH, D_MODEL), 0.02),
        'g_post_mix': 1.0 + normal((DEPTH, D_MODEL), 0.02),
        'g_pre_ffn': 1.0 + normal((DEPTH, D_MODEL), 0.02),
        'g_post_ffn': 1.0 + normal((DEPTH, D_MODEL), 0.02),
    }


def reference(x_prompt, x_sample, state_conv, state_shift, state_rwkv, cache_k, cache_v, page_table,
              w_in, conv_w, mu_shift, w0, w2, a0, a2, g2, k_k, k_a, r_k, ln_w, ln_b, sb_bias,
              w_out, w_up, w_down, g_pre_mix, g_post_mix, g_pre_ffn, g_post_ffn):
    weights = (w_in, conv_w, mu_shift, w0, w2, a0, a2, g2, k_k, k_a, r_k, ln_w, ln_b, sb_bias,
               w_out, w_up, w_down, g_pre_mix, g_post_mix, g_pre_ffn, g_post_ffn)
    Bp = x_prompt.shape[0]
    Bs = x_sample.shape[0]
    conv0 = jnp.zeros((Bp, CONV_WIDTH - 1, SC_WIDTH), x_prompt.dtype)
    shift0 = jnp.zeros((Bp, RW_PROJ), x_prompt.dtype)
    rw0 = jnp.zeros((Bp, RW_HEADS, HEAD_DIM, HEAD_DIM), jnp.float32)
    xp, xs = x_prompt, x_sample
    outs_p, outs_s = [], []
    for l in range(DEPTH):
        lw = [w[l] for w in weights]
        xp, *st_p = decoder_layer(xp, conv0, shift0, rw0, sb_prompt, *lw)
        outs_p.append(st_p)
        past_k = cache_k[l][page_table].reshape(Bs, -1, SB_HEADS, HEAD_DIM)
        past_v = cache_v[l][page_table].reshape(Bs, -1, SB_HEADS, HEAD_DIM)
        attend_s = functools.partial(sb_with_past, past_k=past_k, past_v=past_v)
        xs, *st_s = decoder_layer(xs, state_conv[l], state_shift[l], state_rwkv[l], attend_s, *lw)
        outs_s.append(st_s)
    p_conv, p_shift, p_rwkv, p_k, p_v = (jnp.stack(t) for t in zip(*outs_p))
    s_conv, s_shift, s_rwkv, s_k, s_v = (jnp.stack(t) for t in zip(*outs_s))
    return (xp, xs, p_conv, p_shift, p_rwkv, p_k, p_v, s_conv, s_shift, s_rwkv, s_k, s_v)
```

```python
import functools

import jax
import jax.numpy as jnp
from jax import lax
from jax.experimental import pallas as pl
from jax.experimental.pallas import tpu as pltpu

f32 = jnp.float32
bf16 = jnp.bfloat16

HEAD_DIM = 64
NORM_EPS = 1e-6
RW_GN_EPS = HEAD_DIM * 1e-5
SB_SCALE = HEAD_DIM ** -0.5
V7X_VMEM_LIMIT_BYTES = 56 * 1024 * 1024
ROW_TILE = 512
FF_TILE = 1024
RW_CHUNK = 64
RW_BLOCK = 512
SB_TILE = 256
SAMPLE_GROUP = 8
SUBLANES = 8


def _params(*sem):
    return pltpu.CompilerParams(dimension_semantics=sem or None, vmem_limit_bytes=V7X_VMEM_LIMIT_BYTES)


def _bdot(a, b):
    return jnp.dot(a.astype(bf16), b.astype(bf16), preferred_element_type=f32)


def _dot_nt(a, b):
    return lax.dot_general(a.astype(bf16), b.astype(bf16), (((1,), (1,)), ((), ())), preferred_element_type=f32)


def _dot_tn(a, b):
    return lax.dot_general(a.astype(bf16), b.astype(bf16), (((0,), (0,)), ((), ())), preferred_element_type=f32)


def _split(x, terms):
    parts = []
    for _ in range(terms):
        p = x.astype(bf16)
        parts.append(p)
        x = x - p.astype(f32)
    return parts


def _dot_x01(x, m01, terms=2):
    return sum(jnp.dot(p, m01, preferred_element_type=f32) for p in _split(x, terms))


def _dot_01x(m01, x, terms=2):
    return sum(jnp.dot(m01, p, preferred_element_type=f32) for p in _split(x, terms))


def _rms_rows(x, g):
    return x * lax.rsqrt(jnp.mean(x * x, axis=-1, keepdims=True) + NORM_EPS) * g


def _softplus(x):
    return jnp.maximum(x, 0.0) + jnp.log1p(jnp.exp(-jnp.abs(x)))


def _sigmoid(x):
    return 1.0 / (1.0 + jnp.exp(-x))


def _in_proj_body(x_ref, g_ref, w_ref, *out_refs):
    h = _rms_rows(x_ref[...], g_ref[...]).astype(bf16)
    off = 0
    for o_ref in out_refs:
        n = o_ref.shape[-1]
        o_ref[...] = jnp.dot(h, w_ref[:, off:off + n], preferred_element_type=f32)
        off += n


def in_proj(x, g, w_bf, widths):
    rows, d = x.shape
    tm = min(ROW_TILE, rows)
    assert rows % tm == 0 and sum(widths) == w_bf.shape[1]
    return pl.pallas_call(
        _in_proj_body,
        grid=(rows // tm,),
        in_specs=[pl.BlockSpec((tm, d), lambda i: (i, 0)),
                  pl.BlockSpec((1, d), lambda i: (0, 0)),
                  pl.BlockSpec(w_bf.shape, lambda i: (0, 0))],
        out_specs=[pl.BlockSpec((tm, n), lambda i: (i, 0)) for n in widths],
        out_shape=[jax.ShapeDtypeStruct((rows, n), f32) for n in widths],
        compiler_params=_params("arbitrary"),
        name="in_proj",
    )(x, g.reshape(1, d), w_bf)


def _sconv_seq_body(p_ref, buf_ref, w_ref, y_ref, nb_ref):
    t, w = y_ref.shape
    gate = p_ref[:, 0:w]
    u = p_ref[:, w:2 * w] * p_ref[:, 2 * w:3 * w]
    row = lax.broadcasted_iota(jnp.int32, (t, w), 0)
    buf = buf_ref[...]
    um1 = jnp.where(row == 0, buf[1:2], pltpu.roll(u, 1, 0))
    um2 = jnp.where(row == 0, buf[0:1], jnp.where(row == 1, buf[1:2], pltpu.roll(u, 2, 0)))
    cw = w_ref[...]
    y_ref[...] = gate * (cw[0:1] * um2 + cw[1:2] * um1 + cw[2:3] * u)
    nb_ref[...] = u[t - 2:t]


def sconv_prompt(p_sc, conv_buf, conv_w, batch):
    rows, w3 = p_sc.shape
    t, w = rows // batch, w3 // 3
    assert conv_w.shape[0] == 3 and t >= 2
    return pl.pallas_call(
        _sconv_seq_body,
        grid=(batch,),
        in_specs=[pl.BlockSpec((t, w3), lambda b: (b, 0)),
                  pl.BlockSpec((None, 2, w), lambda b: (b, 0, 0)),
                  pl.BlockSpec((3, w), lambda b: (0, 0))],
        out_specs=[pl.BlockSpec((t, w), lambda b: (b, 0)),
                   pl.BlockSpec((None, 2, w), lambda b: (b, 0, 0))],
        out_shape=[jax.ShapeDtypeStruct((rows, w), f32), jax.ShapeDtypeStruct((batch, 2, w), f32)],
        compiler_params=_params("arbitrary"),
        name="sconv_prompt",
    )(p_sc, conv_buf, conv_w)


def _sconv_tm_body(p_ref, buf_ref, w_ref, y_ref, nb_ref, *, steps):
    rows, w = y_ref.shape
    nb = rows // steps
    gate = p_ref[:, 0:w]
    u = p_ref[:, w:2 * w] * p_ref[:, 2 * w:3 * w]
    pad = [buf_ref[0], buf_ref[1]] + [u[t * nb:(t + 1) * nb] for t in range(steps)]
    cw = w_ref[...]
    for t in range(steps):
        y_ref[t * nb:(t + 1) * nb, :] = gate[t * nb:(t + 1) * nb] * (
            cw[0:1] * pad[t] + cw[1:2] * pad[t + 1] + cw[2:3] * pad[t + 2])
    nb_ref[0] = pad[steps]
    nb_ref[1] = pad[steps + 1]


def sconv_sample(p_sc, conv_buf_tm, conv_w, steps):
    rows, w3 = p_sc.shape
    w = w3 // 3
    return pl.pallas_call(
        functools.partial(_sconv_tm_body, steps=steps),
        out_shape=[jax.ShapeDtypeStruct((rows, w), f32), jax.ShapeDtypeStruct(conv_buf_tm.shape, f32)],
        compiler_params=_params(),
        name="sconv_sample",
    )(p_sc, conv_buf_tm, conv_w)


def _rwkv_prep(p, prev, vec_ref, w2_ref, a2_ref, g2_ref, seg):
    rw = seg.shape[0]
    dl, al = w2_ref.shape[0], a2_ref.shape[0]
    mu = vec_ref[0:1, :]
    w0, a0, kkw, kaw = (vec_ref[i:i + 1, 0:rw] for i in (1, 2, 3, 4))
    xs = p + (prev - p) * mu
    r, k, v = xs[:, 0:rw], xs[:, rw:2 * rw], xs[:, 2 * rw:3 * rw]
    o = 3 * rw
    xw, xa, xg = xs[:, o:o + dl], xs[:, o + dl:o + dl + al], xs[:, o + dl + al:]
    log_w = -_softplus(-(w0 + _bdot(jnp.tanh(xw), w2_ref[...]))) - 0.5
    lw = -jnp.exp(log_w)
    a = _sigmoid(a0 + _bdot(xa, a2_ref[...]))
    g = _bdot(_sigmoid(xg), g2_ref[...])
    kk = k * kkw
    k = k * (1.0 + (a - 1.0) * kaw)
    kk = kk / jnp.maximum(jnp.sqrt(_dot_x01(kk * kk, seg)), 1e-12)
    return r, k, v, kk, a, lw, g


def _rwkv_finish(o, r, k, v, g, vec_ref, seg):
    rw = seg.shape[0]
    rk, lnw, lnb = (vec_ref[i:i + 1, 0:rw] for i in (5, 6, 7))
    inv_n = 1.0 / HEAD_DIM
    mean = _dot_x01(o, seg) * inv_n
    d = o - mean
    var = _dot_x01(d * d, seg) * inv_n
    on = d * lax.rsqrt(var + RW_GN_EPS) * lnw + lnb
    bonus = _dot_x01(r * k * rk, seg) * v
    return (on + bonus) * g


def _rwkv_vectors(mu, w0, a0, k_k, k_a, r_k, ln_w, ln_b):
    proj = mu.shape[-1]
    rows = [mu] + [jnp.pad(t.reshape(-1), (0, proj - t.size)) for t in (w0, a0, k_k, k_a, r_k, ln_w, ln_b)]
    return jnp.stack(rows).astype(f32)


def _seg_ones(width):
    i = jnp.arange(width) // HEAD_DIM
    return (i[:, None] == i[None, :]).astype(bf16)


def _rwkv_prompt_body(p_ref, shift_ref, s0_ref, vec_ref, w2_ref, a2_ref, g2_ref, seg_ref, tri_ref,
                      y_ref, nshift_ref, st_ref,
                      s_scr, last_scr, r_scr, k_scr, v_scr, kk_scr, a_scr, lw_scr, o_scr):
    tb = pl.program_id(1)
    rows, rw = y_ref.shape
    heads = rw // HEAD_DIM
    c = RW_CHUNK

    @pl.when(tb == 0)
    def _():
        s_scr[...] = s0_ref[...]
        last_scr[...] = shift_ref[...]

    p = p_ref[...]
    row = lax.broadcasted_iota(jnp.int32, p.shape, 0)
    prev = jnp.where(row == 0, last_scr[...], pltpu.roll(p, 1, 0))
    last_scr[...] = p[rows - 1:rows]
    nshift_ref[...] = p[rows - 1:rows]
    seg = seg_ref[...]
    r, k, v, kk, a, lw, g = _rwkv_prep(p, prev, vec_ref, w2_ref, a2_ref, g2_ref, seg)
    r_scr[...] = r
    k_scr[...] = k
    v_scr[...] = v
    kk_scr[...] = kk
    a_scr[...] = a
    lw_scr[...] = lw

    def chunk(n, carry):
        ri = lax.broadcasted_iota(jnp.int32, (2 * c, 2 * c), 0)
        ci = lax.broadcasted_iota(jnp.int32, (2 * c, 2 * c), 1) % c
        tri_mask = ci < jnp.where(ri < c, ri, ri - c + 1)
        eye = (lax.broadcasted_iota(jnp.int32, (c, c), 0) == lax.broadcasted_iota(jnp.int32, (c, c), 1)).astype(f32)
        sl_t = pl.ds(pl.multiple_of(n * c, c), c)
        rc, kc, vc, kkc, ac, lwc = (s[sl_t, :] for s in (r_scr, k_scr, v_scr, kk_scr, a_scr, lw_scr))
        cs = _dot_01x(tri_ref[...], lwc, terms=3)
        tot = cs[c - 1:c, :]
        e_neg = jnp.exp(-cs)
        e_rem = jnp.exp(tot - cs)
        kka = kkc * ac
        a_t = (-kkc * jnp.exp(cs - lwc)).astype(bf16)
        r_t = (rc * jnp.exp(cs)).astype(bf16)
        b_t = (kka * e_neg).astype(bf16)
        k_t = (kc * e_neg).astype(bf16)
        b_g = (kka * e_rem).astype(bf16)
        k_g = (kc * e_rem).astype(bf16)
        gdec = jnp.exp(tot)
        vb = vc.astype(bf16)
        for h in range(heads):
            sl = slice(h * HEAD_DIM, (h + 1) * HEAD_DIM)
            ar = jnp.concatenate([a_t[:, sl], r_t[:, sl]], axis=0)
            bk = jnp.concatenate([b_t[:, sl], k_t[:, sl]], axis=0)
            m4 = jnp.where(tri_mask, _dot_nt(ar, bk), 0.0)
            lab = m4[:c, :c]
            pw, tm, span = lab, eye + lab, 1
            while 2 * span < c:
                pw = _bdot(pw, pw)
                tm = tm + _bdot(tm, pw)
                span *= 2
            s_h = s_scr[h]
            x = _dot_nt(ar, s_h) + _bdot(m4[:, c:], vb[:, sl])
            u = _bdot(tm, x[:c])
            o_scr[sl_t, sl] = x[c:] + _bdot(m4[c:, :c], u)
            uv = jnp.concatenate([u.astype(bf16), vb[:, sl]], axis=0)
            bkg = jnp.concatenate([b_g[:, sl], k_g[:, sl]], axis=0)
            s_scr[h] = s_h * gdec[:, sl] + _dot_tn(uv, bkg)
        return carry

    lax.fori_loop(0, rows // c, chunk, 0)
    y_ref[...] = _rwkv_finish(o_scr[...], r, k, v, g, vec_ref, seg)

    @pl.when(tb == pl.num_programs(1) - 1)
    def _():
        st_ref[...] = s_scr[...]


def rwkv_prompt(p_rw, shift_buf, state0, vecs, w2, a2, g2, batch):
    rows, proj = p_rw.shape
    t = rows // batch
    rw = w2.shape[1]
    heads = rw // HEAD_DIM
    blk = min(RW_BLOCK, t)
    assert t % blk == 0 and blk % RW_CHUNK == 0
    nblk = t // blk
    tri = jnp.tril(jnp.ones((RW_CHUNK, RW_CHUNK), bf16))
    const = lambda shape: pl.BlockSpec(shape, lambda b, i: (0,) * len(shape))
    return pl.pallas_call(
        _rwkv_prompt_body,
        grid=(batch, nblk),
        in_specs=[pl.BlockSpec((blk, proj), lambda b, i: (b * nblk + i, 0)),
                  pl.BlockSpec((None, 1, proj), lambda b, i: (b, 0, 0)),
                  pl.BlockSpec((None, heads, HEAD_DIM, HEAD_DIM), lambda b, i: (b, 0, 0, 0)),
                  const(vecs.shape), const(w2.shape), const(a2.shape), const(g2.shape),
                  const((rw, rw)), const(tri.shape)],
        out_specs=[pl.BlockSpec((blk, rw), lambda b, i: (b * nblk + i, 0)),
                   pl.BlockSpec((None, 1, proj), lambda b, i: (b, 0, 0)),
                   pl.BlockSpec((None, heads, HEAD_DIM, HEAD_DIM), lambda b, i: (b, 0, 0, 0))],
        out_shape=[jax.ShapeDtypeStruct((rows, rw), f32),
                   jax.ShapeDtypeStruct((batch, 1, proj), f32),
                   jax.ShapeDtypeStruct(state0.shape, f32)],
        scratch_shapes=[pltpu.VMEM((heads, HEAD_DIM, HEAD_DIM), f32), pltpu.VMEM((1, proj), f32)]
                       + [pltpu.VMEM((blk, rw), f32)] * 7,
        compiler_params=_params("arbitrary", "arbitrary"),
        name="rwkv_prompt",
    )(p_rw, shift_buf.reshape(batch, 1, proj), state0, vecs, w2, a2, g2, _seg_ones(rw), tri)


def _rwkv_sample_body(p_ref, shift_ref, s0_ref, vec_ref, w2_ref, a2_ref, g2_ref, seg_ref,
                      y_ref, nshift_ref, st_ref,
                      r_scr, k_scr, v_scr, kk_scr, kka_scr, dec_scr, o_scr):
    steps, nb, proj = p_ref.shape
    rw = y_ref.shape[-1]
    heads = rw // HEAD_DIM
    p = p_ref[...].reshape(steps * nb, proj)
    prev = jnp.concatenate([shift_ref[...], p[:(steps - 1) * nb]], axis=0)
    nshift_ref[...] = p[(steps - 1) * nb:]
    seg = seg_ref[...]
    r, k, v, kk, a, lw, g = _rwkv_prep(p, prev, vec_ref, w2_ref, a2_ref, g2_ref, seg)
    r_scr[...] = r
    k_scr[...] = k
    v_scr[...] = v
    kk_scr[...] = kk
    kka_scr[...] = kk * a
    dec_scr[...] = jnp.exp(lw)
    n = HEAD_DIM
    eye = (lax.broadcasted_iota(jnp.int32, (n, n), 0) == lax.broadcasted_iota(jnp.int32, (n, n), 1)).astype(f32)

    def one_sequence(i, carry):
        at = [pl.ds(t * nb + i, 1) for t in range(steps)]
        rows = [[scr[a, :] for scr in (r_scr, k_scr, v_scr, kk_scr, kka_scr, dec_scr)] for a in at]
        o_parts = [[] for _ in range(steps)]
        for h in range(heads):
            sl = slice(h * n, (h + 1) * n)
            s = s0_ref[i, h]
            for t in range(steps):
                r_t, k_t, v_t, kk_t, kka_t, dec_t = (x[:, sl] for x in rows[t])
                sa = -jnp.sum(s * kk_t, axis=1, keepdims=True)
                v_col = jnp.sum(eye * v_t, axis=1, keepdims=True)
                s = s * dec_t + sa * kka_t + v_col * k_t
                o_col = jnp.sum(s * r_t, axis=1, keepdims=True)
                o_parts[t].append(jnp.sum(eye * o_col, axis=0, keepdims=True))
            st_ref[i, h] = s
        for t in range(steps):
            o_scr[at[t], :] = jnp.concatenate(o_parts[t], axis=1)
        return carry

    lax.fori_loop(0, nb, one_sequence, 0)
    y_ref[...] = _rwkv_finish(o_scr[...], r, k, v, g, vec_ref, seg).reshape(steps, nb, rw)


def rwkv_sample(p_rw, shift_buf, state0, vecs, w2, a2, g2, steps):
    rows, proj = p_rw.shape
    nseq = rows // steps
    rw = w2.shape[1]
    heads = rw // HEAD_DIM
    nb = SAMPLE_GROUP
    assert nseq % nb == 0
    const = lambda shape: pl.BlockSpec(shape, lambda j: (0,) * len(shape))
    y, nshift, st = pl.pallas_call(
        _rwkv_sample_body,
        grid=(nseq // nb,),
        in_specs=[pl.BlockSpec((steps, nb, proj), lambda j: (0, j, 0)),
                  pl.BlockSpec((nb, proj), lambda j: (j, 0)),
                  pl.BlockSpec((nb, heads, HEAD_DIM, HEAD_DIM), lambda j: (j, 0, 0, 0)),
                  const(vecs.shape), const(w2.shape), const(a2.shape), const(g2.shape), const((rw, rw))],
        out_specs=[pl.BlockSpec((steps, nb, rw), lambda j: (0, j, 0)),
                   pl.BlockSpec((nb, proj), lambda j: (j, 0)),
                   pl.BlockSpec((nb, heads, HEAD_DIM, HEAD_DIM), lambda j: (j, 0, 0, 0))],
        out_shape=[jax.ShapeDtypeStruct((steps, nseq, rw), f32),
                   jax.ShapeDtypeStruct((nseq, proj), f32),
                   jax.ShapeDtypeStruct(state0.shape, f32)],
        scratch_shapes=[pltpu.VMEM((steps * nb, rw), f32)] * 7,
        compiler_params=_params("arbitrary"),
        name="rwkv_sample",
    )(p_rw.reshape(steps, nseq, proj), shift_buf, state0, vecs, w2, a2, g2, _seg_ones(rw))
    return y.reshape(rows, rw), nshift, st


def _sb_prompt_body(bias_ref, q_ref, k_ref, v_ref, m_ref, o_ref):
    i = pl.program_id(1)
    tq, width = q_ref.shape
    heads = width // HEAD_DIM
    tk = tq
    below = lax.broadcasted_iota(jnp.int32, (tq, tk), 1) < lax.broadcasted_iota(jnp.int32, (tq, tk), 0)
    m_later = m_ref[...]

    for h in range(heads):
        sl = slice(h * HEAD_DIM, (h + 1) * HEAD_DIM)
        qh = (q_ref[:, sl] * SB_SCALE).astype(bf16)
        bias = bias_ref[h]

        def tile(kj, carry, acc, diagonal):
            at = pl.ds(pl.multiple_of(kj * tk, tk), tk)
            z = _dot_nt(qh, k_ref[at, sl]) + bias
            l1m = -_softplus(z)
            log_beta = z + l1m
            if diagonal:
                l1m = jnp.where(below, l1m, 0.0)
            w = jnp.exp(log_beta + _dot_x01(l1m, m_later) + carry)
            if diagonal:
                w = jnp.where(below, w, 0.0)
            acc = acc + _bdot(w, v_ref[at, sl])
            return carry + jnp.sum(l1m, axis=1, keepdims=True), acc

        carry, acc = tile(i, jnp.zeros((tq, 1), f32), jnp.zeros((tq, HEAD_DIM), f32), True)
        carry, acc = lax.fori_loop(0, i, lambda n, c: tile(i - 1 - n, c[0], c[1], False), (carry, acc))
        o_ref[:, sl] = acc


def sb_prompt(q, k, v, bias, batch):
    rows, width = q.shape
    t = rows // batch
    tq = min(SB_TILE, t)
    assert t % tq == 0
    nq = t // tq
    idx = jnp.arange(tq)
    m_later = (idx[:, None] > idx[None, :]).astype(bf16)
    return pl.pallas_call(
        _sb_prompt_body,
        grid=(batch, nq),
        in_specs=[pl.BlockSpec(memory_space=pltpu.SMEM),
                  pl.BlockSpec((tq, width), lambda b, i: (b * nq + i, 0)),
                  pl.BlockSpec((t, width), lambda b, i: (b, 0)),
                  pl.BlockSpec((t, width), lambda b, i: (b, 0)),
                  pl.BlockSpec((tq, tq), lambda b, i: (0, 0))],
        out_specs=pl.BlockSpec((tq, width), lambda b, i: (b * nq + i, 0)),
        out_shape=jax.ShapeDtypeStruct((rows, width), f32),
        compiler_params=_params("arbitrary", "arbitrary"),
        name="sb_prompt",
    )(bias.astype(f32), q, k, v, m_later)


def _sb_sample_body(pt_ref, q_ref, kn_ref, vn_ref, bias_ref, u_ref, *rest, pages, steps):
    k_refs, v_refs, o_ref = rest[:pages], rest[pages:2 * pages], rest[2 * pages]
    del pt_ref
    width = q_ref.shape[-1]
    heads = width // HEAD_DIM
    page = u_ref.shape[0]
    cols = page
    q = q_ref[...] * SB_SCALE
    lane_head = lax.broadcasted_iota(jnp.int32, q.shape, 1) // HEAD_DIM
    qt = jnp.concatenate([jnp.where(lane_head == h, q, 0.0) for h in range(heads)]
                         + [jnp.zeros((cols - SUBLANES * heads, width), f32)], axis=0).astype(bf16)
    bias = bias_ref[...]

    z = _dot_nt(kn_ref[...], qt) + bias
    kj = lax.broadcasted_iota(jnp.int32, z.shape, 0)
    qi = lax.broadcasted_iota(jnp.int32, z.shape, 1) % SUBLANES
    seen = kj < qi
    l1m = jnp.where(seen, -_softplus(z), 0.0)
    log_beta = z - _softplus(z)
    later = [jnp.zeros((1, cols), f32)]
    for j in range(steps - 1, 0, -1):
        later.append(later[-1] + l1m[j:j + 1])
    suffix = jnp.concatenate(later[::-1] + [jnp.zeros((SUBLANES - steps, cols), f32)], axis=0)
    w = jnp.where(seen, jnp.exp(log_beta + suffix), 0.0)
    acc = _dot_tn(w, vn_ref[...])
    carry = jnp.sum(l1m, axis=0, keepdims=True)

    u_later = u_ref[...]
    for j in range(pages - 1, -1, -1):
        z = _dot_nt(k_refs[j][...], qt) + bias
        l1m = -_softplus(z)
        w = jnp.exp(z + l1m + _dot_01x(u_later, l1m) + carry)
        acc = acc + _dot_tn(w, v_refs[j][...])
        carry = carry + jnp.sum(l1m, axis=0, keepdims=True)

    o_ref[...] = jnp.concatenate(
        [acc[h * SUBLANES:h * SUBLANES + steps, h * HEAD_DIM:(h + 1) * HEAD_DIM] for h in range(heads)], axis=1)


def sb_sample(q, k_new, v_new, bias, cache_k, cache_v, page_table, layer):
    nseq, steps, width = q.shape
    heads = width // HEAD_DIM
    pages = page_table.shape[1]
    page = cache_k.shape[2]
    assert steps <= SUBLANES and heads * SUBLANES <= page
    pad8 = lambda t: jnp.pad(t, ((0, 0), (0, SUBLANES - steps), (0, 0)))
    bias_cols = jnp.pad(jnp.repeat(bias.astype(f32), SUBLANES), (0, page - heads * SUBLANES)).reshape(1, page)
    idx = jnp.arange(page)
    u_later = (idx[None, :] > idx[:, None]).astype(bf16)
    row8 = pl.BlockSpec((None, SUBLANES, width), lambda b, pt: (b, 0, 0))
    page_spec = lambda j: pl.BlockSpec((None, None, page, width), lambda b, pt: (layer, pt[b, j], 0, 0))
    grid_spec = pltpu.PrefetchScalarGridSpec(
        num_scalar_prefetch=1,
        grid=(nseq,),
        in_specs=[row8, row8, row8,
                  pl.BlockSpec((1, page), lambda b, pt: (0, 0)),
                  pl.BlockSpec((page, page), lambda b, pt: (0, 0))]
                 + [page_spec(j) for j in range(pages)] * 2,
        out_specs=pl.BlockSpec((None, steps, width), lambda b, pt: (b, 0, 0)),
    )
    return pl.pallas_call(
        functools.partial(_sb_sample_body, pages=pages, steps=steps),
        grid_spec=grid_spec,
        out_shape=jax.ShapeDtypeStruct((nseq, steps, width), f32),
        compiler_params=_params("arbitrary"),
        name="sb_sample",
    )(page_table, pad8(q), pad8(k_new), pad8(v_new), bias_cols, u_later,
      *([cache_k] * pages), *([cache_v] * pages))


def _out_ffn_body(x_ref, ysc_ref, yrw_ref, ysb_ref, wout_ref, wup_ref, wdown_ref, gains_ref,
                  o_ref, x1_scr, f_scr, acc_scr):
    j = pl.program_id(1)

    @pl.when(j == 0)
    def _():
        off, mix = 0, None
        for y_ref in (ysc_ref, yrw_ref, ysb_ref):
            n = y_ref.shape[-1]
            part = jnp.dot(y_ref[...].astype(bf16), wout_ref[off:off + n, :], preferred_element_type=f32)
            mix = part if mix is None else mix + part
            off += n
        x1 = x_ref[...] + _rms_rows(mix, gains_ref[0:1, :])
        x1_scr[...] = x1
        f_scr[...] = _rms_rows(x1, gains_ref[1:2, :]).astype(bf16)
        acc_scr[...] = jnp.zeros_like(acc_scr)

    hid = jnp.maximum(jnp.dot(f_scr[...], wup_ref[...], preferred_element_type=f32), 0.0)
    acc_scr[...] += jnp.dot((hid * hid).astype(bf16), wdown_ref[...], preferred_element_type=f32)

    @pl.when(j == pl.num_programs(1) - 1)
    def _():
        o_ref[...] = x1_scr[...] + _rms_rows(acc_scr[...], gains_ref[2:3, :])


def out_ffn(x, y_sc, y_rw, y_sb, w_out_bf, w_up_bf, w_down_bf, gains):
    rows, d = x.shape
    dff = w_up_bf.shape[1]
    tm = min(ROW_TILE, rows)
    tf = min(FF_TILE, dff)
    assert rows % tm == 0 and dff % tf == 0
    row_spec = lambda n: pl.BlockSpec((tm, n), lambda i, j: (i, 0))
    return pl.pallas_call(
        _out_ffn_body,
        grid=(rows // tm, dff // tf),
        in_specs=[row_spec(d), row_spec(y_sc.shape[1]), row_spec(y_rw.shape[1]), row_spec(y_sb.shape[1]),
                  pl.BlockSpec((d, d), lambda i, j: (0, 0)),
                  pl.BlockSpec((d, tf), lambda i, j: (0, j)),
                  pl.BlockSpec((tf, d), lambda i, j: (j, 0)),
                  pl.BlockSpec(gains.shape, lambda i, j: (0, 0))],
        out_specs=row_spec(d),
        out_shape=jax.ShapeDtypeStruct((rows, d), f32),
        scratch_shapes=[pltpu.VMEM((tm, d), f32), pltpu.VMEM((tm, d), bf16), pltpu.VMEM((tm, d), f32)],
        compiler_params=_params("arbitrary", "arbitrary"),
        name="out_ffn",
    )(x, y_sc, y_rw, y_sb, w_out_bf, w_up_bf, w_down_bf, gains)


def kernel(x_prompt, x_sample, state_conv, state_shift, state_rwkv, cache_k, cache_v, page_table, w_in, conv_w, mu_shift, w0, w2, a0, a2, g2, k_k, k_a, r_k, ln_w, ln_b, sb_bias, w_out, w_up, w_down, g_pre_mix, g_post_mix, g_pre_ffn, g_post_ffn):
    depth = w_in.shape[0]
    bp, tp, d = x_prompt.shape
    bs, ts, _ = x_sample.shape
    sc_w = conv_w.shape[-1]
    rw_w = w0.shape[-1]
    rw_proj = mu_shift.shape[-1]
    sb_heads = sb_bias.shape[-1]
    sb_w = sb_heads * HEAD_DIM
    rw_heads = rw_w // HEAD_DIM
    widths = (3 * sc_w, rw_proj, sb_w, sb_w, sb_w)
    assert sum(widths) == w_in.shape[-1]

    w_in_bf, w_out_bf, w_up_bf, w_down_bf = (w.astype(bf16) for w in (w_in, w_out, w_up, w_down))
    cache_k4 = cache_k.reshape(cache_k.shape[:3] + (sb_w,))
    cache_v4 = cache_v.reshape(cache_v.shape[:3] + (sb_w,))

    xp = x_prompt.reshape(bp * tp, d)
    xs = jnp.swapaxes(x_sample, 0, 1).reshape(ts * bs, d)
    conv0 = jnp.zeros((bp, 2, sc_w), f32)
    shift0 = jnp.zeros((bp, rw_proj), f32)
    rw0 = jnp.zeros((bp, rw_heads, HEAD_DIM, HEAD_DIM), f32)

    outs_p, outs_s = [], []
    for l in range(depth):
        vecs = _rwkv_vectors(mu_shift[l], w0[l], a0[l], k_k[l], k_a[l], r_k[l], ln_w[l], ln_b[l])
        gains = jnp.stack([g_post_mix[l], g_pre_ffn[l], g_post_ffn[l]])
        lora = (w2[l], a2[l], g2[l])

        p_sc, p_rw, q, k, v = in_proj(xp, g_pre_mix[l], w_in_bf[l], widths)
        y_sc, n_conv = sconv_prompt(p_sc, conv0, conv_w[l], bp)
        y_rw, n_shift, n_state = rwkv_prompt(p_rw, shift0, rw0, vecs, *lora, bp)
        y_sb = sb_prompt(q, k, v, sb_bias[l], bp)
        xp = out_ffn(xp, y_sc, y_rw, y_sb, w_out_bf[l], w_up_bf[l], w_down_bf[l], gains)
        outs_p.append((n_conv, n_shift.reshape(bp, rw_proj), n_state,
                       k.reshape(bp, tp, sb_heads, HEAD_DIM), v.reshape(bp, tp, sb_heads, HEAD_DIM)))

        p_sc, p_rw, q, k, v = in_proj(xs, g_pre_mix[l], w_in_bf[l], widths)
        y_sc, n_conv = sconv_sample(p_sc, jnp.swapaxes(state_conv[l], 0, 1), conv_w[l], ts)
        y_rw, n_shift, n_state = rwkv_sample(p_rw, state_shift[l], state_rwkv[l], vecs, *lora, ts)
        to_bm = lambda t: jnp.swapaxes(t.reshape(ts, bs, sb_w), 0, 1)
        y_sb = sb_sample(to_bm(q), to_bm(k), to_bm(v), sb_bias[l], cache_k4, cache_v4, page_table, l)
        y_sb = jnp.swapaxes(y_sb, 0, 1).reshape(ts * bs, sb_w)
        xs = out_ffn(xs, y_sc, y_rw, y_sb, w_out_bf[l], w_up_bf[l], w_down_bf[l], gains)
        outs_s.append((jnp.swapaxes(n_conv, 0, 1), n_shift, n_state,
                       to_bm(k).reshape(bs, ts, sb_heads, HEAD_DIM), to_bm(v).reshape(bs, ts, sb_heads, HEAD_DIM)))

    p_conv, p_shift, p_rwkv, p_k, p_v = (jnp.stack(t) for t in zip(*outs_p))
    s_conv, s_shift, s_rwkv, s_k, s_v = (jnp.stack(t) for t in zip(*outs_s))
    y_prompt = xp.reshape(bp, tp, d)
    y_sample = jnp.swapaxes(xs.reshape(ts, bs, d), 0, 1)
    return (y_prompt, y_sample, p_conv, p_shift, p_rwkv, p_k, p_v, s_conv, s_shift, s_rwkv, s_k, s_v)
```

```python
import functools

import jax
import jax.numpy as jnp
from jax import lax
from jax.experimental import pallas as pl
from jax.experimental.pallas import tpu as pltpu

f32 = jnp.float32
bf16 = jnp.bfloat16

HEAD_DIM = 64
NORM_EPS = 1e-6
RW_GN_EPS = HEAD_DIM * 1e-5
SB_SCALE = HEAD_DIM ** -0.5
LOG2_E = 1.4426950408889634
V7X_VMEM_LIMIT_BYTES = 56 * 1024 * 1024
V7X_MXU_LANES = 256
ROW_TILE = 512
FF_TILE = 1024
RW_CHUNK = 64
RW_BLOCK = 512
RW_PREP_GROUP = 4
SB_PROMPT_SPLIT_TERMS = 1
SB_SAMPLE_SPLIT_TERMS = 2
SB_TILE = 256
SAMPLE_GROUP = 8
SUBLANES = 8


def _params(*sem):
    return pltpu.CompilerParams(dimension_semantics=sem or None, vmem_limit_bytes=V7X_VMEM_LIMIT_BYTES)


def _bdot(a, b):
    return jnp.dot(a.astype(bf16), b.astype(bf16), preferred_element_type=f32)


def _dot_nt(a, b):
    return lax.dot_general(a.astype(bf16), b.astype(bf16), (((1,), (1,)), ((), ())), preferred_element_type=f32)


def _dot_tn(a, b):
    return lax.dot_general(a.astype(bf16), b.astype(bf16), (((0,), (0,)), ((), ())), preferred_element_type=f32)


def _split(x, terms):
    parts = []
    for _ in range(terms):
        p = x.astype(bf16)
        parts.append(p)
        x = x - p.astype(f32)
    return parts


def _dot_x01(x, m01, terms=2):
    return sum(jnp.dot(p, m01, preferred_element_type=f32) for p in _split(x, terms))


def _dot_01x(m01, x, terms=2):
    return sum(jnp.dot(m01, p, preferred_element_type=f32) for p in _split(x, terms))


def _rms_rows(x, g):
    return x * lax.rsqrt(jnp.mean(x * x, axis=-1, keepdims=True) + NORM_EPS) * g


def _softplus(x):
    return jnp.maximum(x, 0.0) + jnp.log(1.0 + jnp.exp(-jnp.abs(x)))


def _log2_one_minus_beta(n2):
    neg_abs = lax.bitcast_convert_type(lax.bitcast_convert_type(n2, jnp.uint32) | jnp.uint32(0x80000000), f32)
    return jnp.minimum(n2, 0.0) - jnp.log2(1.0 + jnp.exp2(neg_abs))


def _sigmoid(x):
    return 1.0 / (1.0 + jnp.exp(-x))


def _lane_groups(width):
    return [(g, min(g + V7X_MXU_LANES, width)) for g in range(0, width, V7X_MXU_LANES)]


def _head_blocks(x, heads):
    c = x.shape[0]
    t = jnp.concatenate([x] * heads, axis=0)
    keep = (lax.broadcasted_iota(jnp.int32, t.shape, 0) // c) == (lax.broadcasted_iota(jnp.int32, t.shape, 1) // HEAD_DIM)
    return jnp.where(keep, t, jnp.zeros_like(t))


def _bd_dot(lhs, x):
    outs = []
    for g0, g1 in _lane_groups(x.shape[1]):
        xb = _head_blocks(x[:, g0:g1].astype(bf16), (g1 - g0) // HEAD_DIM)
        outs.append(jnp.dot(lhs[:, g0:g1].astype(bf16), xb, preferred_element_type=f32))
    return jnp.concatenate(outs, axis=1)


def _bd_dot_nt(lhs, x):
    outs = []
    for g0, g1 in _lane_groups(x.shape[1]):
        xb = _head_blocks(x[:, g0:g1].astype(bf16), (g1 - g0) // HEAD_DIM)
        outs.append(_dot_nt(lhs[:, g0:g1], xb))
    return jnp.concatenate(outs, axis=1)


def _in_proj_body(x_ref, g_ref, w_ref, *out_refs):
    h = _rms_rows(x_ref[...], g_ref[...]).astype(bf16)
    off = 0
    for o_ref in out_refs:
        n = o_ref.shape[-1]
        o_ref[...] = jnp.dot(h, w_ref[:, off:off + n], preferred_element_type=f32)
        off += n


def in_proj(x, g, w_bf, widths):
    rows, d = x.shape
    tm = min(ROW_TILE, rows)
    assert rows % tm == 0 and sum(widths) == w_bf.shape[1]
    return pl.pallas_call(
        _in_proj_body,
        grid=(rows // tm,),
        in_specs=[pl.BlockSpec((tm, d), lambda i: (i, 0)),
                  pl.BlockSpec((1, d), lambda i: (0, 0)),
                  pl.BlockSpec(w_bf.shape, lambda i: (0, 0))],
        out_specs=[pl.BlockSpec((tm, n), lambda i: (i, 0)) for n in widths],
        out_shape=[jax.ShapeDtypeStruct((rows, n), f32) for n in widths],
        compiler_params=_params("arbitrary"),
        name="in_proj",
    )(x, g.reshape(1, d), w_bf)


def _sconv_seq_body(p_ref, buf_ref, w_ref, y_ref, nb_ref):
    t, w = y_ref.shape
    gate = p_ref[:, 0:w]
    u = p_ref[:, w:2 * w] * p_ref[:, 2 * w:3 * w]
    row = lax.broadcasted_iota(jnp.int32, (t, w), 0)
    buf = buf_ref[...]
    um1 = jnp.where(row == 0, buf[1:2], pltpu.roll(u, 1, 0))
    um2 = jnp.where(row == 0, buf[0:1], jnp.where(row == 1, buf[1:2], pltpu.roll(u, 2, 0)))
    cw = w_ref[...]
    y_ref[...] = gate * (cw[0:1] * um2 + cw[1:2] * um1 + cw[2:3] * u)
    nb_ref[...] = u[t - 2:t]


def sconv_prompt(p_sc, conv_buf, conv_w, batch):
    rows, w3 = p_sc.shape
    t, w = rows // batch, w3 // 3
    assert conv_w.shape[0] == 3 and t >= 2
    return pl.pallas_call(
        _sconv_seq_body,
        grid=(batch,),
        in_specs=[pl.BlockSpec((t, w3), lambda b: (b, 0)),
                  pl.BlockSpec((None, 2, w), lambda b: (b, 0, 0)),
                  pl.BlockSpec((3, w), lambda b: (0, 0))],
        out_specs=[pl.BlockSpec((t, w), lambda b: (b, 0)),
                   pl.BlockSpec((None, 2, w), lambda b: (b, 0, 0))],
        out_shape=[jax.ShapeDtypeStruct((rows, w), f32), jax.ShapeDtypeStruct((batch, 2, w), f32)],
        compiler_params=_params("arbitrary"),
        name="sconv_prompt",
    )(p_sc, conv_buf, conv_w)


def _sconv_tm_body(p_ref, buf_ref, w_ref, y_ref, nb_ref, *, steps):
    rows, w = y_ref.shape
    nb = rows // steps
    gate = p_ref[:, 0:w]
    u = p_ref[:, w:2 * w] * p_ref[:, 2 * w:3 * w]
    pad = [buf_ref[0], buf_ref[1]] + [u[t * nb:(t + 1) * nb] for t in range(steps)]
    cw = w_ref[...]
    for t in range(steps):
        y_ref[t * nb:(t + 1) * nb, :] = gate[t * nb:(t + 1) * nb] * (
            cw[0:1] * pad[t] + cw[1:2] * pad[t + 1] + cw[2:3] * pad[t + 2])
    nb_ref[0] = pad[steps]
    nb_ref[1] = pad[steps + 1]


def sconv_sample(p_sc, conv_buf_tm, conv_w, steps):
    rows, w3 = p_sc.shape
    w = w3 // 3
    return pl.pallas_call(
        functools.partial(_sconv_tm_body, steps=steps),
        out_shape=[jax.ShapeDtypeStruct((rows, w), f32), jax.ShapeDtypeStruct(conv_buf_tm.shape, f32)],
        compiler_params=_params(),
        name="sconv_sample",
    )(p_sc, conv_buf_tm, conv_w)


def _rwkv_prep(p, prev, vec_ref, w2_ref, a2_ref, g2_ref, seg):
    rw = seg.shape[0]
    dl, al = w2_ref.shape[0], a2_ref.shape[0]
    mu = vec_ref[0:1, :]
    w0, a0, kkw, kaw = (vec_ref[i:i + 1, 0:rw] for i in (1, 2, 3, 4))
    xs = p + (prev - p) * mu
    r, k, v = xs[:, 0:rw], xs[:, rw:2 * rw], xs[:, 2 * rw:3 * rw]
    o = 3 * rw
    xw, xa, xg = xs[:, o:o + dl], xs[:, o + dl:o + dl + al], xs[:, o + dl + al:]
    log_w = -_softplus(-(w0 + _bdot(jnp.tanh(xw), w2_ref[...]))) - 0.5
    lw = -jnp.exp(log_w)
    a = _sigmoid(a0 + _bdot(xa, a2_ref[...]))
    g = _bdot(_sigmoid(xg), g2_ref[...])
    kk = k * kkw
    k = k * (1.0 + (a - 1.0) * kaw)
    kk = kk * jnp.minimum(lax.rsqrt(_dot_x01(kk * kk, seg)), 1e12)
    return r, k, v, kk, a, lw, g


def _rwkv_finish(o, r, k, v, g, vec_ref, seg):
    rw = seg.shape[0]
    rk, lnw, lnb = (vec_ref[i:i + 1, 0:rw] for i in (5, 6, 7))
    inv_n = 1.0 / HEAD_DIM
    mean = _dot_x01(o, seg) * inv_n
    d = o - mean
    var = _dot_x01(d * d, seg) * inv_n
    on = d * lax.rsqrt(var + RW_GN_EPS) * lnw + lnb
    bonus = _dot_x01(r * k * rk, seg) * v
    return (on + bonus) * g


def _rwkv_vectors(mu, w0, a0, k_k, k_a, r_k, ln_w, ln_b):
    proj = mu.shape[-1]
    rows = [mu] + [jnp.pad(t.reshape(-1), (0, proj - t.size)) for t in (w0, a0, k_k, k_a, r_k, ln_w, ln_b)]
    return jnp.stack(rows).astype(f32)


def _seg_ones(width):
    i = jnp.arange(width) // HEAD_DIM
    return (i[:, None] == i[None, :]).astype(bf16)


def _rwkv_prompt_body(p_ref, shift_ref, s0_ref, vec_ref, w2_ref, a2_ref, g2_ref, seg_ref, tri_ref,
                      y_ref, nshift_ref, st_ref,
                      s_scr, last_scr, r_scr, k_scr, v_scr, kk_scr, a_scr, lw_scr, o_scr,
                      ar_scr, tm_scr, lv_scr, arb_scr, bkg_scr, g_scr):
    tb = pl.program_id(1)
    rows, rw = y_ref.shape
    heads = rw // HEAD_DIM
    c = RW_CHUNK
    nchunk = rows // c
    hsl = lambda h: slice(h * HEAD_DIM, (h + 1) * HEAD_DIM)

    @pl.when(tb == 0)
    def _():
        s_scr[...] = jnp.zeros_like(s_scr)
        for h in range(heads):
            s_scr[hsl(h), hsl(h)] = s0_ref[h]
        last_scr[...] = shift_ref[...]

    p = p_ref[...]
    row = lax.broadcasted_iota(jnp.int32, p.shape, 0)
    prev = jnp.where(row == 0, last_scr[...], pltpu.roll(p, 1, 0))
    last_scr[...] = p[rows - 1:rows]
    nshift_ref[...] = p[rows - 1:rows]
    seg = seg_ref[...]
    r, k, v, kk, a, lw, g = _rwkv_prep(p, prev, vec_ref, w2_ref, a2_ref, g2_ref, seg)
    r_scr[...] = r
    k_scr[...] = k
    v_scr[...] = v
    kk_scr[...] = kk
    a_scr[...] = a
    lw_scr[...] = lw

    group = RW_PREP_GROUP if nchunk % RW_PREP_GROUP == 0 else 1

    def prepare(n, carry):
        ri = lax.broadcasted_iota(jnp.int32, (2 * c, rw), 0)
        cj = lax.broadcasted_iota(jnp.int32, (2 * c, rw), 1) % HEAD_DIM
        tri_mask = cj < jnp.where(ri < c, ri, ri - c + 1)
        eye = (lax.broadcasted_iota(jnp.int32, (c, rw), 0)
               == lax.broadcasted_iota(jnp.int32, (c, rw), 1) % HEAD_DIM).astype(f32)
        gs = range(group)
        sl_t = [pl.ds(pl.multiple_of((n * group + i) * c, c), c) for i in gs]
        sl_2t = [pl.ds(pl.multiple_of((n * group + i) * 2 * c, 2 * c), 2 * c) for i in gs]
        ld = lambda scr: [scr[sl_t[i], :] for i in gs]
        rc, kc, vc, kkc, ac, lwc = (ld(s) for s in (r_scr, k_scr, v_scr, kk_scr, a_scr, lw_scr))
        cs = [_dot_01x(tri_ref[...], lwc[i], terms=3) for i in gs]
        tot = [cs[i][c - 1:c, :] for i in gs]
        e_neg = [jnp.exp(-cs[i]) for i in gs]
        e_rem = [jnp.exp(tot[i] - cs[i]) for i in gs]
        kka = [kkc[i] * ac[i] for i in gs]
        ar = [jnp.concatenate([-kkc[i] * jnp.exp(cs[i] - lwc[i]), rc[i] * jnp.exp(cs[i])], axis=0).astype(bf16)
              for i in gs]
        b_t = [(kka[i] * e_neg[i]).astype(bf16) for i in gs]
        k_t = [(kc[i] * e_neg[i]).astype(bf16) for i in gs]
        m_b = [jnp.where(tri_mask, _bd_dot_nt(ar[i], b_t[i]), 0.0) for i in gs]
        m_k = [jnp.where(tri_mask, _bd_dot_nt(ar[i], k_t[i]), 0.0) for i in gs]
        pw = [m_b[i][:c] for i in gs]
        tm = [eye + pw[i] for i in gs]
        span = 1
        while 2 * span < c:
            pw = [_bd_dot(pw[i], pw[i]) for i in gs]
            tm = [tm[i] + _bd_dot(tm[i], pw[i]) for i in gs]
            span *= 2
        lv = [_bd_dot(m_k[i], vc[i]) for i in gs]
        for i in gs:
            ar_scr[sl_2t[i], :] = ar[i]
            tm_scr[sl_t[i], :] = tm[i].astype(bf16)
            lv_scr[sl_2t[i], :] = lv[i]
            arb_scr[sl_t[i], :] = m_b[i][c:].astype(bf16)
            bkg_scr[sl_2t[i], :] = jnp.concatenate([kka[i] * e_rem[i], kc[i] * e_rem[i]], axis=0).astype(bf16)
            g_scr[pl.ds(n * group + i, 1), :] = jnp.exp(tot[i])
        return carry

    lax.fori_loop(0, nchunk // group, prepare, 0)

    def advance(n, carry):
        sl_t = pl.ds(pl.multiple_of(n * c, c), c)
        sl_2t = pl.ds(pl.multiple_of(n * 2 * c, 2 * c), 2 * c)
        ar = ar_scr[sl_2t, :]
        x = jnp.concatenate([_dot_nt(ar[:, g0:g1], s_scr[g0:g1, g0:g1]) for g0, g1 in _lane_groups(rw)], axis=1)
        x = x + lv_scr[sl_2t, :]
        u = _bd_dot(tm_scr[sl_t, :], x[:c])
        o_scr[sl_t, :] = x[c:] + _bd_dot(arb_scr[sl_t, :], u)
        uv = jnp.concatenate([u.astype(bf16), v_scr[sl_t, :].astype(bf16)], axis=0)
        bkg = bkg_scr[sl_2t, :]
        gdec = g_scr[pl.ds(n, 1), :]
        for g0, g1 in _lane_groups(rw):
            upd = _dot_tn(uv[:, g0:g1], bkg[:, g0:g1])
            same_head = (lax.broadcasted_iota(jnp.int32, upd.shape, 0) // HEAD_DIM
                         == lax.broadcasted_iota(jnp.int32, upd.shape, 1) // HEAD_DIM)
            s_scr[g0:g1, g0:g1] = s_scr[g0:g1, g0:g1] * gdec[:, g0:g1] + jnp.where(same_head, upd, 0.0)
        return carry

    lax.fori_loop(0, nchunk, advance, 0)
    y_ref[...] = _rwkv_finish(o_scr[...], r, k, v, g, vec_ref, seg)

    @pl.when(tb == pl.num_programs(1) - 1)
    def _():
        for h in range(heads):
            st_ref[h] = s_scr[hsl(h), hsl(h)]


def rwkv_prompt(p_rw, shift_buf, state0, vecs, w2, a2, g2, batch):
    rows, proj = p_rw.shape
    t = rows // batch
    rw = w2.shape[1]
    heads = rw // HEAD_DIM
    blk = min(RW_BLOCK, t)
    assert t % blk == 0 and blk % RW_CHUNK == 0 and RW_CHUNK == HEAD_DIM
    nblk = t // blk
    nchunk = blk // RW_CHUNK
    tri = jnp.tril(jnp.ones((RW_CHUNK, RW_CHUNK), bf16))
    const = lambda shape: pl.BlockSpec(shape, lambda b, i: (0,) * len(shape))
    return pl.pallas_call(
        _rwkv_prompt_body,
        grid=(batch, nblk),
        in_specs=[pl.BlockSpec((blk, proj), lambda b, i: (b * nblk + i, 0)),
                  pl.BlockSpec((None, 1, proj), lambda b, i: (b, 0, 0)),
                  pl.BlockSpec((None, heads, HEAD_DIM, HEAD_DIM), lambda b, i: (b, 0, 0, 0)),
                  const(vecs.shape), const(w2.shape), const(a2.shape), const(g2.shape),
                  const((rw, rw)), const(tri.shape)],
        out_specs=[pl.BlockSpec((blk, rw), lambda b, i: (b * nblk + i, 0)),
                   pl.BlockSpec((None, 1, proj), lambda b, i: (b, 0, 0)),
                   pl.BlockSpec((None, heads, HEAD_DIM, HEAD_DIM), lambda b, i: (b, 0, 0, 0))],
        out_shape=[jax.ShapeDtypeStruct((rows, rw), f32),
                   jax.ShapeDtypeStruct((batch, 1, proj), f32),
                   jax.ShapeDtypeStruct(state0.shape, f32)],
        scratch_shapes=[pltpu.VMEM((rw, rw), f32), pltpu.VMEM((1, proj), f32)]
                       + [pltpu.VMEM((blk, rw), f32)] * 7
                       + [pltpu.VMEM((2 * blk, rw), bf16), pltpu.VMEM((blk, rw), bf16), pltpu.VMEM((2 * blk, rw), f32),
                          pltpu.VMEM((blk, rw), bf16), pltpu.VMEM((2 * blk, rw), bf16), pltpu.VMEM((nchunk, rw), f32)],
        compiler_params=_params("arbitrary", "arbitrary"),
        name="rwkv_prompt",
    )(p_rw, shift_buf.reshape(batch, 1, proj), state0, vecs, w2, a2, g2, _seg_ones(rw), tri)


def _rwkv_sample_body(p_ref, shift_ref, s0_ref, vec_ref, w2_ref, a2_ref, g2_ref, seg_ref,
                      y_ref, nshift_ref, st_ref,
                      r_scr, k_scr, v_scr, kk_scr, kka_scr, dec_scr, o_scr):
    steps, nb, proj = p_ref.shape
    rw = y_ref.shape[-1]
    heads = rw // HEAD_DIM
    p = p_ref[...].reshape(steps * nb, proj)
    prev = jnp.concatenate([shift_ref[...], p[:(steps - 1) * nb]], axis=0)
    nshift_ref[...] = p[(steps - 1) * nb:]
    seg = seg_ref[...]
    r, k, v, kk, a, lw, g = _rwkv_prep(p, prev, vec_ref, w2_ref, a2_ref, g2_ref, seg)
    r_scr[...] = r
    k_scr[...] = k
    v_scr[...] = v
    kk_scr[...] = kk
    kka_scr[...] = kk * a
    dec_scr[...] = jnp.exp(lw)
    n = HEAD_DIM
    hs = range(heads)
    sl = [slice(h * n, (h + 1) * n) for h in hs]

    def one_sequence(i, carry):
        eye = (lax.broadcasted_iota(jnp.int32, (n, n), 0) == lax.broadcasted_iota(jnp.int32, (n, n), 1)).astype(f32)
        s = [s0_ref[i, h] for h in hs]
        for t in range(steps):
            at = pl.ds(t * nb + i, 1)
            r_t, k_t, v_t, kk_t, kka_t, dec_t = (scr[at, :] for scr in (r_scr, k_scr, v_scr, kk_scr, kka_scr, dec_scr))
            sa = [-jnp.sum(s[h] * kk_t[:, sl[h]], axis=1, keepdims=True) for h in hs]
            v_col = [jnp.sum(eye * v_t[:, sl[h]], axis=1, keepdims=True) for h in hs]
            s = [s[h] * dec_t[:, sl[h]] + sa[h] * kka_t[:, sl[h]] + v_col[h] * k_t[:, sl[h]] for h in hs]
            o_col = [jnp.sum(s[h] * r_t[:, sl[h]], axis=1, keepdims=True) for h in hs]
            o_scr[at, :] = jnp.concatenate([jnp.sum(eye * o_col[h], axis=0, keepdims=True) for h in hs], axis=1)
        for h in hs:
            st_ref[i, h] = s[h]
        return carry

    lax.fori_loop(0, nb, one_sequence, 0)
    y_ref[...] = _rwkv_finish(o_scr[...], r, k, v, g, vec_ref, seg).reshape(steps, nb, rw)


def rwkv_sample(p_rw, shift_buf, state0, vecs, w2, a2, g2, steps):
    rows, proj = p_rw.shape
    nseq = rows // steps
    rw = w2.shape[1]
    heads = rw // HEAD_DIM
    nb = SAMPLE_GROUP
    assert nseq % nb == 0
    const = lambda shape: pl.BlockSpec(shape, lambda j: (0,) * len(shape))
    y, nshift, st = pl.pallas_call(
        _rwkv_sample_body,
        grid=(nseq // nb,),
        in_specs=[pl.BlockSpec((steps, nb, proj), lambda j: (0, j, 0)),
                  pl.BlockSpec((nb, proj), lambda j: (j, 0)),
                  pl.BlockSpec((nb, heads, HEAD_DIM, HEAD_DIM), lambda j: (j, 0, 0, 0)),
                  const(vecs.shape), const(w2.shape), const(a2.shape), const(g2.shape), const((rw, rw))],
        out_specs=[pl.BlockSpec((steps, nb, rw), lambda j: (0, j, 0)),
                   pl.BlockSpec((nb, proj), lambda j: (j, 0)),
                   pl.BlockSpec((nb, heads, HEAD_DIM, HEAD_DIM), lambda j: (j, 0, 0, 0))],
        out_shape=[jax.ShapeDtypeStruct((steps, nseq, rw), f32),
                   jax.ShapeDtypeStruct((nseq, proj), f32),
                   jax.ShapeDtypeStruct(state0.shape, f32)],
        scratch_shapes=[pltpu.VMEM((steps * nb, rw), f32)] * 7,
        compiler_params=_params("arbitrary"),
        name="rwkv_sample",
    )(p_rw.reshape(steps, nseq, proj), shift_buf, state0, vecs, w2, a2, g2, _seg_ones(rw))
    return y.reshape(rows, rw), nshift, st


def _later_matrix(n, terms):
    idx = jnp.arange(n)
    m_later = (idx[:, None] > idx[None, :]).astype(bf16)
    return jnp.concatenate([m_later] * terms, axis=0)


def _sb_prompt_body(bias_ref, q_ref, k_ref, v_ref, m_ref, o_ref):
    i = pl.program_id(1)
    tq, width = q_ref.shape
    heads = width // HEAD_DIM
    tk = tq
    terms = m_ref.shape[0] // tk
    m_later = m_ref[...]
    hs = range(heads)
    sl = [slice(h * HEAD_DIM, (h + 1) * HEAD_DIM) for h in hs]
    qn = [(q_ref[:, sl[h]] * (-SB_SCALE * LOG2_E)).astype(bf16) for h in hs]

    def tile(kj, carries, diagonal):
        at = pl.ds(pl.multiple_of(kj * tk, tk), tk)
        if diagonal:
            below = lax.broadcasted_iota(jnp.int32, (tq, tk), 1) < lax.broadcasted_iota(jnp.int32, (tq, tk), 0)
        n2 = [_dot_nt(qn[h], k_ref[at, sl[h]]) - bias_ref[h] for h in hs]
        l1m = [_log2_one_minus_beta(n2[h]) for h in hs]
        log_beta = [l1m[h] - n2[h] for h in hs]
        if diagonal:
            l1m = [jnp.where(below, l1m[h], 0.0) for h in hs]
        parts = [jnp.concatenate(_split(l1m[h], terms), axis=1) for h in hs]
        suffix = [jnp.dot(parts[h], m_later, preferred_element_type=f32) for h in hs]
        w = [jnp.exp2(log_beta[h] + suffix[h] + carries[h]) for h in hs]
        if diagonal:
            w = [jnp.where(below, w[h], 0.0) for h in hs]
        out = [_bdot(w[h], v_ref[at, sl[h]]) for h in hs]
        for h in hs:
            if diagonal:
                o_ref[:, sl[h]] = out[h]
            else:
                o_ref[:, sl[h]] += out[h]
        return tuple(carries[h] + jnp.sum(l1m[h], axis=1, keepdims=True) for h in hs)

    carries = tile(i, tuple(jnp.zeros((tq, 1), f32) for _ in hs), True)
    lax.fori_loop(0, i, lambda t, c: tile(i - 1 - t, c, False), carries)


def sb_prompt(q, k, v, bias, batch):
    rows, width = q.shape
    t = rows // batch
    tq = min(SB_TILE, t)
    assert t % tq == 0
    nq = t // tq
    m_later = _later_matrix(tq, SB_PROMPT_SPLIT_TERMS)
    return pl.pallas_call(
        _sb_prompt_body,
        grid=(batch, nq),
        in_specs=[pl.BlockSpec(memory_space=pltpu.SMEM),
                  pl.BlockSpec((tq, width), lambda b, i: (b * nq + i, 0)),
                  pl.BlockSpec((t, width), lambda b, i: (b, 0)),
                  pl.BlockSpec((t, width), lambda b, i: (b, 0)),
                  pl.BlockSpec(m_later.shape, lambda b, i: (0, 0))],
        out_specs=pl.BlockSpec((tq, width), lambda b, i: (b * nq + i, 0)),
        out_shape=jax.ShapeDtypeStruct((rows, width), f32),
        compiler_params=_params("arbitrary", "arbitrary"),
        name="sb_prompt",
    )(bias.astype(f32) * LOG2_E, q, k, v, m_later)


def _sb_sample_body(pt_ref, q_ref, kn_ref, vn_ref, bias_ref, m_ref, *rest, pages, steps):
    k_refs, v_refs, o_ref = rest[:pages], rest[pages:2 * pages], rest[2 * pages]
    del pt_ref
    width = q_ref.shape[-1]
    heads = width // HEAD_DIM
    page = m_ref.shape[1]
    terms = m_ref.shape[0] // page
    rows = heads * SUBLANES
    qn = q_ref[...] * (-SB_SCALE * LOG2_E)
    lane_head = lax.broadcasted_iota(jnp.int32, qn.shape, 1) // HEAD_DIM
    qm = jnp.concatenate([jnp.where(lane_head == h, qn, 0.0) for h in range(heads)], axis=0).astype(bf16)
    fill = jnp.zeros((page - SUBLANES, width), f32)
    k_all = jnp.concatenate([r[...].astype(bf16) for r in k_refs]
                            + [jnp.concatenate([kn_ref[...], fill], axis=0).astype(bf16)], axis=0)
    n2 = _dot_nt(qm, k_all) - bias_ref[...]
    l1m = _log2_one_minus_beta(n2)
    log_beta = l1m - n2
    seen = (lax.broadcasted_iota(jnp.int32, (rows, page), 1)
            < lax.broadcasted_iota(jnp.int32, (rows, page), 0) % SUBLANES)

    m_later = m_ref[...]
    carry = jnp.zeros((rows, 1), f32)
    w_parts = [None] * (pages + 1)
    for j in range(pages, -1, -1):
        at = slice(j * page, (j + 1) * page)
        lj = l1m[:, at]
        if j == pages:
            lj = jnp.where(seen, lj, 0.0)
        suffix = jnp.dot(jnp.concatenate(_split(lj, terms), axis=1), m_later, preferred_element_type=f32)
        w = jnp.exp2(log_beta[:, at] + suffix + carry)
        if j == pages:
            w = jnp.where(seen, w, 0.0)
        w_parts[j] = w.astype(bf16)
        carry = carry + jnp.sum(lj, axis=1, keepdims=True)
    w_all = jnp.concatenate(w_parts, axis=1)
    v_all = jnp.concatenate([r[...].astype(bf16) for r in v_refs]
                            + [jnp.concatenate([vn_ref[...], fill], axis=0).astype(bf16)], axis=0)
    acc = jnp.dot(w_all, v_all, preferred_element_type=f32)
    o_ref[...] = jnp.concatenate(
        [acc[h * SUBLANES:h * SUBLANES + steps, h * HEAD_DIM:(h + 1) * HEAD_DIM] for h in range(heads)], axis=1)


def sb_sample(q, k_new, v_new, bias, cache_k, cache_v, page_table, layer):
    nseq, steps, width = q.shape
    heads = width // HEAD_DIM
    pages = page_table.shape[1]
    page = cache_k.shape[2]
    assert steps <= SUBLANES
    pad8 = lambda t: jnp.pad(t, ((0, 0), (0, SUBLANES - steps), (0, 0)))
    bias_rows = jnp.repeat(bias.astype(f32) * LOG2_E, SUBLANES).reshape(heads * SUBLANES, 1)
    m_later = _later_matrix(page, SB_SAMPLE_SPLIT_TERMS)
    row8 = pl.BlockSpec((None, SUBLANES, width), lambda b, pt: (b, 0, 0))
    page_spec = lambda j: pl.BlockSpec((None, None, page, width), lambda b, pt: (layer, pt[b, j], 0, 0))
    grid_spec = pltpu.PrefetchScalarGridSpec(
        num_scalar_prefetch=1,
        grid=(nseq,),
        in_specs=[row8, row8, row8,
                  pl.BlockSpec(bias_rows.shape, lambda b, pt: (0, 0)),
                  pl.BlockSpec(m_later.shape, lambda b, pt: (0, 0))]
                 + [page_spec(j) for j in range(pages)] * 2,
        out_specs=pl.BlockSpec((None, steps, width), lambda b, pt: (b, 0, 0)),
    )
    return pl.pallas_call(
        functools.partial(_sb_sample_body, pages=pages, steps=steps),
        grid_spec=grid_spec,
        out_shape=jax.ShapeDtypeStruct((nseq, steps, width), f32),
        compiler_params=_params("arbitrary"),
        name="sb_sample",
    )(page_table, pad8(q), pad8(k_new), pad8(v_new), bias_rows, m_later,
      *([cache_k] * pages), *([cache_v] * pages))


def _out_ffn_body(x_ref, ysc_ref, yrw_ref, ysb_ref, wout_ref, wup_ref, wdown_ref, gains_ref,
                  o_ref, x1_scr, f_scr, acc_scr):
    j = pl.program_id(1)

    @pl.when(j == 0)
    def _():
        off, mix = 0, None
        for y_ref in (ysc_ref, yrw_ref, ysb_ref):
            n = y_ref.shape[-1]
            part = jnp.dot(y_ref[...].astype(bf16), wout_ref[off:off + n, :], preferred_element_type=f32)
            mix = part if mix is None else mix + part
            off += n
        x1 = x_ref[...] + _rms_rows(mix, gains_ref[0:1, :])
        x1_scr[...] = x1
        f_scr[...] = _rms_rows(x1, gains_ref[1:2, :]).astype(bf16)
        acc_scr[...] = jnp.zeros_like(acc_scr)

    hid = jnp.maximum(jnp.dot(f_scr[...], wup_ref[...], preferred_element_type=f32), 0.0)
    acc_scr[...] += jnp.dot((hid * hid).astype(bf16), wdown_ref[...], preferred_element_type=f32)

    @pl.when(j == pl.num_programs(1) - 1)
    def _():
        o_ref[...] = x1_scr[...] + _rms_rows(acc_scr[...], gains_ref[2:3, :])


def out_ffn(x, y_sc, y_rw, y_sb, w_out_bf, w_up_bf, w_down_bf, gains):
    rows, d = x.shape
    dff = w_up_bf.shape[1]
    tm = min(ROW_TILE, rows)
    tf = min(FF_TILE, dff)
    assert rows % tm == 0 and dff % tf == 0
    row_spec = lambda n: pl.BlockSpec((tm, n), lambda i, j: (i, 0))
    return pl.pallas_call(
        _out_ffn_body,
        grid=(rows // tm, dff // tf),
        in_specs=[row_spec(d), row_spec(y_sc.shape[1]), row_spec(y_rw.shape[1]), row_spec(y_sb.shape[1]),
                  pl.BlockSpec((d, d), lambda i, j: (0, 0)),
                  pl.BlockSpec((d, tf), lambda i, j: (0, j)),
                  pl.BlockSpec((tf, d), lambda i, j: (j, 0)),
                  pl.BlockSpec(gains.shape, lambda i, j: (0, 0))],
        out_specs=row_spec(d),
        out_shape=jax.ShapeDtypeStruct((rows, d), f32),
        scratch_shapes=[pltpu.VMEM((tm, d), f32), pltpu.VMEM((tm, d), bf16), pltpu.VMEM((tm, d), f32)],
        compiler_params=_params("arbitrary", "arbitrary"),
        name="out_ffn",
    )(x, y_sc, y_rw, y_sb, w_out_bf, w_up_bf, w_down_bf, gains)


def kernel(x_prompt, x_sample, state_conv, state_shift, state_rwkv, cache_k, cache_v, page_table, w_in, conv_w, mu_shift, w0, w2, a0, a2, g2, k_k, k_a, r_k, ln_w, ln_b, sb_bias, w_out, w_up, w_down, g_pre_mix, g_post_mix, g_pre_ffn, g_post_ffn):
    depth = w_in.shape[0]
    bp, tp, d = x_prompt.shape
    bs, ts, _ = x_sample.shape
    sc_w = conv_w.shape[-1]
    rw_w = w0.shape[-1]
    rw_proj = mu_shift.shape[-1]
    sb_heads = sb_bias.shape[-1]
    sb_w = sb_heads * HEAD_DIM
    rw_heads = rw_w // HEAD_DIM
    widths = (3 * sc_w, rw_proj, sb_w, sb_w, sb_w)
    assert sum(widths) == w_in.shape[-1]

    w_in_bf, w_out_bf, w_up_bf, w_down_bf = (w.astype(bf16) for w in (w_in, w_out, w_up, w_down))
    cache_k4 = cache_k.reshape(cache_k.shape[:3] + (sb_w,))
    cache_v4 = cache_v.reshape(cache_v.shape[:3] + (sb_w,))

    xp = x_prompt.reshape(bp * tp, d)
    xs = jnp.swapaxes(x_sample, 0, 1).reshape(ts * bs, d)
    conv0 = jnp.zeros((bp, 2, sc_w), f32)
    shift0 = jnp.zeros((bp, rw_proj), f32)
    rw0 = jnp.zeros((bp, rw_heads, HEAD_DIM, HEAD_DIM), f32)

    outs_p, outs_s = [], []
    for l in range(depth):
        vecs = _rwkv_vectors(mu_shift[l], w0[l], a0[l], k_k[l], k_a[l], r_k[l], ln_w[l], ln_b[l])
        gains = jnp.stack([g_post_mix[l], g_pre_ffn[l], g_post_ffn[l]])
        lora = (w2[l], a2[l], g2[l])

        p_sc, p_rw, q, k, v = in_proj(xp, g_pre_mix[l], w_in_bf[l], widths)
        y_sc, n_conv = sconv_prompt(p_sc, conv0, conv_w[l], bp)
        y_rw, n_shift, n_state = rwkv_prompt(p_rw, shift0, rw0, vecs, *lora, bp)
        y_sb = sb_prompt(q, k, v, sb_bias[l], bp)
        xp = out_ffn(xp, y_sc, y_rw, y_sb, w_out_bf[l], w_up_bf[l], w_down_bf[l], gains)
        outs_p.append((n_conv, n_shift.reshape(bp, rw_proj), n_state,
                       k.reshape(bp, tp, sb_heads, HEAD_DIM), v.reshape(bp, tp, sb_heads, HEAD_DIM)))

        p_sc, p_rw, q, k, v = in_proj(xs, g_pre_mix[l], w_in_bf[l], widths)
        y_sc, n_conv = sconv_sample(p_sc, jnp.swapaxes(state_conv[l], 0, 1), conv_w[l], ts)
        y_rw, n_shift, n_state = rwkv_sample(p_rw, state_shift[l], state_rwkv[l], vecs, *lora, ts)
        to_bm = lambda t: jnp.swapaxes(t.reshape(ts, bs, sb_w), 0, 1)
        y_sb = sb_sample(to_bm(q), to_bm(k), to_bm(v), sb_bias[l], cache_k4, cache_v4, page_table, l)
        y_sb = jnp.swapaxes(y_sb, 0, 1).reshape(ts * bs, sb_w)
        xs = out_ffn(xs, y_sc, y_rw, y_sb, w_out_bf[l], w_up_bf[l], w_down_bf[l], gains)
        outs_s.append((jnp.swapaxes(n_conv, 0, 1), n_shift, n_state,
                       to_bm(k).reshape(bs, ts, sb_heads, HEAD_DIM), to_bm(v).reshape(bs, ts, sb_heads, HEAD_DIM)))

    p_conv, p_shift, p_rwkv, p_k, p_v = (jnp.stack(t) for t in zip(*outs_p))
    s_conv, s_shift, s_rwkv, s_k, s_v = (jnp.stack(t) for t in zip(*outs_s))
    y_prompt = xp.reshape(bp, tp, d)
    y_sample = jnp.swapaxes(xs.reshape(ts, bs, d), 0, 1)
    return (y_prompt, y_sample, p_conv, p_shift, p_rwkv, p_k, p_v, s_conv, s_shift, s_rwkv, s_k, s_v)
```

```python
import functools

import jax
import jax.numpy as jnp
from jax import lax
from jax.experimental import pallas as pl
from jax.experimental.pallas import tpu as pltpu

f32 = jnp.float32
bf16 = jnp.bfloat16

HEAD_DIM = 64
NORM_EPS = 1e-6
RW_GN_EPS = HEAD_DIM * 1e-5
SB_SCALE = HEAD_DIM ** -0.5
LOG2_E = 1.4426950408889634
V7X_VMEM_LIMIT_BYTES = 56 * 1024 * 1024
V7X_MXU_LANES = 256
ROW_TILE = 512
FF_TILE = 1024
RW_CHUNK = 64
RW_BLOCK = 512
RW_PREP_GROUP = 8
SB_PROMPT_SPLIT_TERMS = 1
SB_SAMPLE_SPLIT_TERMS = 2
SB_TILE = 256
SAMPLE_GROUP = 8
SUBLANES = 8


def _params(*sem):
    return pltpu.CompilerParams(dimension_semantics=sem or None, vmem_limit_bytes=V7X_VMEM_LIMIT_BYTES)


def _bdot(a, b):
    return jnp.dot(a.astype(bf16), b.astype(bf16), preferred_element_type=f32)


def _dot_nt(a, b):
    return lax.dot_general(a.astype(bf16), b.astype(bf16), (((1,), (1,)), ((), ())), preferred_element_type=f32)


def _dot_tn(a, b):
    return lax.dot_general(a.astype(bf16), b.astype(bf16), (((0,), (0,)), ((), ())), preferred_element_type=f32)


def _split(x, terms):
    parts = []
    for _ in range(terms):
        p = x.astype(bf16)
        parts.append(p)
        x = x - p.astype(f32)
    return parts


def _dot_x01(x, m01, terms=2):
    return sum(jnp.dot(p, m01, preferred_element_type=f32) for p in _split(x, terms))


def _dot_01x(m01, x, terms=2):
    return sum(jnp.dot(m01, p, preferred_element_type=f32) for p in _split(x, terms))


def _rms_rows(x, g):
    return x * lax.rsqrt(jnp.mean(x * x, axis=-1, keepdims=True) + NORM_EPS) * g


def _softplus(x):
    return jnp.maximum(x, 0.0) + jnp.log(1.0 + jnp.exp(-jnp.abs(x)))


def _log2_one_minus_beta(n2):
    return jnp.minimum(n2, 0.0) - jnp.log2(1.0 + jnp.exp2(-jnp.abs(n2)))


def _sigmoid(x):
    return 1.0 / (1.0 + jnp.exp(-x))


def _lane_groups(width):
    return [(g, min(g + V7X_MXU_LANES, width)) for g in range(0, width, V7X_MXU_LANES)]


def _head_blocks(x, heads):
    c = x.shape[0]
    t = jnp.concatenate([x] * heads, axis=0)
    keep = (lax.broadcasted_iota(jnp.int32, t.shape, 0) // c) == (lax.broadcasted_iota(jnp.int32, t.shape, 1) // HEAD_DIM)
    return jnp.where(keep, t, jnp.zeros_like(t))


def _bd_dot(lhs, x):
    outs = []
    for g0, g1 in _lane_groups(x.shape[1]):
        xb = _head_blocks(x[:, g0:g1].astype(bf16), (g1 - g0) // HEAD_DIM)
        outs.append(jnp.dot(lhs[:, g0:g1].astype(bf16), xb, preferred_element_type=f32))
    return jnp.concatenate(outs, axis=1)


def _bd_dot_nt(lhs, x):
    outs = []
    for g0, g1 in _lane_groups(x.shape[1]):
        xb = _head_blocks(x[:, g0:g1].astype(bf16), (g1 - g0) // HEAD_DIM)
        outs.append(_dot_nt(lhs[:, g0:g1], xb))
    return jnp.concatenate(outs, axis=1)


def _in_proj_body(x_ref, g_ref, w_ref, *out_refs):
    h = _rms_rows(x_ref[...], g_ref[...]).astype(bf16)
    off = 0
    for o_ref in out_refs:
        n = o_ref.shape[-1]
        o_ref[...] = jnp.dot(h, w_ref[:, off:off + n], preferred_element_type=f32)
        off += n


def in_proj(x, g, w_bf, widths):
    rows, d = x.shape
    tm = min(ROW_TILE, rows)
    assert rows % tm == 0 and sum(widths) == w_bf.shape[1]
    return pl.pallas_call(
        _in_proj_body,
        grid=(rows // tm,),
        in_specs=[pl.BlockSpec((tm, d), lambda i: (i, 0)),
                  pl.BlockSpec((1, d), lambda i: (0, 0)),
                  pl.BlockSpec(w_bf.shape, lambda i: (0, 0))],
        out_specs=[pl.BlockSpec((tm, n), lambda i: (i, 0)) for n in widths],
        out_shape=[jax.ShapeDtypeStruct((rows, n), f32) for n in widths],
        compiler_params=_params("arbitrary"),
        name="in_proj",
    )(x, g.reshape(1, d), w_bf)


def _sconv_seq_body(p_ref, buf_ref, w_ref, y_ref, nb_ref):
    t, w = y_ref.shape
    gate = p_ref[:, 0:w]
    u = p_ref[:, w:2 * w] * p_ref[:, 2 * w:3 * w]
    row = lax.broadcasted_iota(jnp.int32, (t, w), 0)
    buf = buf_ref[...]
    um1 = jnp.where(row == 0, buf[1:2], pltpu.roll(u, 1, 0))
    um2 = jnp.where(row == 0, buf[0:1], jnp.where(row == 1, buf[1:2], pltpu.roll(u, 2, 0)))
    cw = w_ref[...]
    y_ref[...] = gate * (cw[0:1] * um2 + cw[1:2] * um1 + cw[2:3] * u)
    nb_ref[...] = u[t - 2:t]


def sconv_prompt(p_sc, conv_buf, conv_w, batch):
    rows, w3 = p_sc.shape
    t, w = rows // batch, w3 // 3
    assert conv_w.shape[0] == 3 and t >= 2
    return pl.pallas_call(
        _sconv_seq_body,
        grid=(batch,),
        in_specs=[pl.BlockSpec((t, w3), lambda b: (b, 0)),
                  pl.BlockSpec((None, 2, w), lambda b: (b, 0, 0)),
                  pl.BlockSpec((3, w), lambda b: (0, 0))],
        out_specs=[pl.BlockSpec((t, w), lambda b: (b, 0)),
                   pl.BlockSpec((None, 2, w), lambda b: (b, 0, 0))],
        out_shape=[jax.ShapeDtypeStruct((rows, w), f32), jax.ShapeDtypeStruct((batch, 2, w), f32)],
        compiler_params=_params("arbitrary"),
        name="sconv_prompt",
    )(p_sc, conv_buf, conv_w)


def _sconv_tm_body(p_ref, buf_ref, w_ref, y_ref, nb_ref, *, steps):
    rows, w = y_ref.shape
    nb = rows // steps
    gate = p_ref[:, 0:w]
    u = p_ref[:, w:2 * w] * p_ref[:, 2 * w:3 * w]
    pad = [buf_ref[0], buf_ref[1]] + [u[t * nb:(t + 1) * nb] for t in range(steps)]
    cw = w_ref[...]
    for t in range(steps):
        y_ref[t * nb:(t + 1) * nb, :] = gate[t * nb:(t + 1) * nb] * (
            cw[0:1] * pad[t] + cw[1:2] * pad[t + 1] + cw[2:3] * pad[t + 2])
    nb_ref[0] = pad[steps]
    nb_ref[1] = pad[steps + 1]


def sconv_sample(p_sc, conv_buf_tm, conv_w, steps):
    rows, w3 = p_sc.shape
    w = w3 // 3
    return pl.pallas_call(
        functools.partial(_sconv_tm_body, steps=steps),
        out_shape=[jax.ShapeDtypeStruct((rows, w), f32), jax.ShapeDtypeStruct(conv_buf_tm.shape, f32)],
        compiler_params=_params(),
        name="sconv_sample",
    )(p_sc, conv_buf_tm, conv_w)


def _rwkv_prep(p, prev, vec_ref, w2_ref, a2_ref, g2_ref, seg):
    rw = seg.shape[0]
    dl, al = w2_ref.shape[0], a2_ref.shape[0]
    mu = vec_ref[0:1, :]
    w0, a0, kkw, kaw = (vec_ref[i:i + 1, 0:rw] for i in (1, 2, 3, 4))
    xs = p + (prev - p) * mu
    r, k, v = xs[:, 0:rw], xs[:, rw:2 * rw], xs[:, 2 * rw:3 * rw]
    o = 3 * rw
    xw, xa, xg = xs[:, o:o + dl], xs[:, o + dl:o + dl + al], xs[:, o + dl + al:]
    log_w = -_softplus(-(w0 + _bdot(jnp.tanh(xw), w2_ref[...]))) - 0.5
    lw = -jnp.exp(log_w)
    a = _sigmoid(a0 + _bdot(xa, a2_ref[...]))
    g = _bdot(_sigmoid(xg), g2_ref[...])
    kk = k * kkw
    k = k * (1.0 + (a - 1.0) * kaw)
    kk = kk * jnp.minimum(lax.rsqrt(_dot_x01(kk * kk, seg, terms=1)), 1e12)
    return r, k, v, kk, a, lw, g


def _rwkv_finish(o, r, k, v, g, vec_ref, seg):
    rw = seg.shape[0]
    rk, lnw, lnb = (vec_ref[i:i + 1, 0:rw] for i in (5, 6, 7))
    inv_n = 1.0 / HEAD_DIM
    mean = _dot_x01(o, seg) * inv_n
    d = o - mean
    var = _dot_x01(d * d, seg, terms=1) * inv_n
    on = d * lax.rsqrt(var + RW_GN_EPS) * lnw + lnb
    bonus = _dot_x01(r * k * rk, seg, terms=1) * v
    return (on + bonus) * g


def _rwkv_vectors(mu, w0, a0, k_k, k_a, r_k, ln_w, ln_b):
    proj = mu.shape[-1]
    rows = [mu] + [jnp.pad(t.reshape(-1), (0, proj - t.size)) for t in (w0, a0, k_k, k_a, r_k, ln_w, ln_b)]
    return jnp.stack(rows).astype(f32)


def _seg_ones(width):
    i = jnp.arange(width) // HEAD_DIM
    return (i[:, None] == i[None, :]).astype(bf16)


def _rwkv_prompt_body(p_ref, shift_ref, s0_ref, vec_ref, w2_ref, a2_ref, g2_ref, seg_ref, tri_ref,
                      y_ref, nshift_ref, st_ref,
                      s_scr, last_scr, r_scr, k_scr, v_scr, kk_scr, a_scr, lw_scr, o_scr,
                      ar_scr, tm_scr, lv_scr, arb_scr, bkg_scr, g_scr):
    tb = pl.program_id(1)
    rows, rw = y_ref.shape
    heads = rw // HEAD_DIM
    c = RW_CHUNK
    nchunk = rows // c
    hsl = lambda h: slice(h * HEAD_DIM, (h + 1) * HEAD_DIM)

    @pl.when(tb == 0)
    def _():
        s_scr[...] = jnp.zeros_like(s_scr)
        for h in range(heads):
            s_scr[hsl(h), hsl(h)] = s0_ref[h]
        last_scr[...] = shift_ref[...]

    p = p_ref[...]
    row = lax.broadcasted_iota(jnp.int32, p.shape, 0)
    prev = jnp.where(row == 0, last_scr[...], pltpu.roll(p, 1, 0))
    last_scr[...] = p[rows - 1:rows]
    nshift_ref[...] = p[rows - 1:rows]
    seg = seg_ref[...]
    r, k, v, kk, a, lw, g = _rwkv_prep(p, prev, vec_ref, w2_ref, a2_ref, g2_ref, seg)
    r_scr[...] = r
    k_scr[...] = k
    v_scr[...] = v
    kk_scr[...] = kk
    a_scr[...] = a
    lw_scr[...] = lw

    group = RW_PREP_GROUP if nchunk % RW_PREP_GROUP == 0 else 1

    def prepare(n, carry):
        ri = lax.broadcasted_iota(jnp.int32, (2 * c, rw), 0)
        cj = lax.broadcasted_iota(jnp.int32, (2 * c, rw), 1) % HEAD_DIM
        tri_mask = cj < jnp.where(ri < c, ri, ri - c + 1)
        eye = (lax.broadcasted_iota(jnp.int32, (c, rw), 0)
               == lax.broadcasted_iota(jnp.int32, (c, rw), 1) % HEAD_DIM).astype(f32)
        gs = range(group)
        sl_t = [pl.ds(pl.multiple_of((n * group + i) * c, c), c) for i in gs]
        sl_2t = [pl.ds(pl.multiple_of((n * group + i) * 2 * c, 2 * c), 2 * c) for i in gs]
        ld = lambda scr: [scr[sl_t[i], :] for i in gs]
        rc, kc, vc, kkc, ac, lwc = (ld(s) for s in (r_scr, k_scr, v_scr, kk_scr, a_scr, lw_scr))
        cs = [_dot_01x(tri_ref[...], lwc[i], terms=3) for i in gs]
        tot = [cs[i][c - 1:c, :] for i in gs]
        e_neg = [jnp.exp(-cs[i]) for i in gs]
        e_rem = [jnp.exp(tot[i] - cs[i]) for i in gs]
        kka = [kkc[i] * ac[i] for i in gs]
        ar = [jnp.concatenate([-kkc[i] * jnp.exp(cs[i] - lwc[i]), rc[i] * jnp.exp(cs[i])], axis=0).astype(bf16)
              for i in gs]
        b_t = [(kka[i] * e_neg[i]).astype(bf16) for i in gs]
        k_t = [(kc[i] * e_neg[i]).astype(bf16) for i in gs]
        m_b = [jnp.where(tri_mask, _bd_dot_nt(ar[i], b_t[i]), 0.0) for i in gs]
        m_k = [jnp.where(tri_mask, _bd_dot_nt(ar[i], k_t[i]), 0.0) for i in gs]
        pw = [m_b[i][:c] for i in gs]
        tm = [eye + pw[i] for i in gs]
        span = 1
        while 2 * span < c:
            pw = [_bd_dot(pw[i], pw[i]) for i in gs]
            tm = [tm[i] + _bd_dot(tm[i], pw[i]) for i in gs]
            span *= 2
        lv = [_bd_dot(m_k[i], vc[i]) for i in gs]
        for i in gs:
            ar_scr[sl_2t[i], :] = ar[i]
            tm_scr[sl_t[i], :] = tm[i].astype(bf16)
            lv_scr[sl_2t[i], :] = lv[i]
            arb_scr[sl_t[i], :] = m_b[i][c:].astype(bf16)
            bkg_scr[sl_2t[i], :] = jnp.concatenate([kka[i] * e_rem[i], kc[i] * e_rem[i]], axis=0).astype(bf16)
            g_scr[pl.ds(n * group + i, 1), :] = jnp.exp(tot[i])
        return carry

    lax.fori_loop(0, nchunk // group, prepare, 0)

    def advance(n, carry):
        sl_t = pl.ds(pl.multiple_of(n * c, c), c)
        sl_2t = pl.ds(pl.multiple_of(n * 2 * c, 2 * c), 2 * c)
        ar = ar_scr[sl_2t, :]
        x = jnp.concatenate([_dot_nt(ar[:, g0:g1], s_scr[g0:g1, g0:g1]) for g0, g1 in _lane_groups(rw)], axis=1)
        x = x + lv_scr[sl_2t, :]
        u = _bd_dot(tm_scr[sl_t, :], x[:c])
        o_scr[sl_t, :] = x[c:] + _bd_dot(arb_scr[sl_t, :], u)
        uv = jnp.concatenate([u.astype(bf16), v_scr[sl_t, :].astype(bf16)], axis=0)
        bkg = bkg_scr[sl_2t, :]
        gdec = g_scr[pl.ds(n, 1), :]
        for g0, g1 in _lane_groups(rw):
            upd = _dot_tn(uv[:, g0:g1], bkg[:, g0:g1])
            same_head = (lax.broadcasted_iota(jnp.int32, upd.shape, 0) // HEAD_DIM
                         == lax.broadcasted_iota(jnp.int32, upd.shape, 1) // HEAD_DIM)
            s_scr[g0:g1, g0:g1] = s_scr[g0:g1, g0:g1] * gdec[:, g0:g1] + jnp.where(same_head, upd, 0.0)
        return carry

    lax.fori_loop(0, nchunk, advance, 0)
    y_ref[...] = _rwkv_finish(o_scr[...], r, k, v, g, vec_ref, seg)

    @pl.when(tb == pl.num_programs(1) - 1)
    def _():
        for h in range(heads):
            st_ref[h] = s_scr[hsl(h), hsl(h)]


def rwkv_prompt(p_rw, shift_buf, state0, vecs, w2, a2, g2, batch):
    rows, proj = p_rw.shape
    t = rows // batch
    rw = w2.shape[1]
    heads = rw // HEAD_DIM
    blk = min(RW_BLOCK, t)
    assert t % blk == 0 and blk % RW_CHUNK == 0 and RW_CHUNK == HEAD_DIM
    nblk = t // blk
    nchunk = blk // RW_CHUNK
    tri = jnp.tril(jnp.ones((RW_CHUNK, RW_CHUNK), bf16))
    const = lambda shape: pl.BlockSpec(shape, lambda b, i: (0,) * len(shape))
    return pl.pallas_call(
        _rwkv_prompt_body,
        grid=(batch, nblk),
        in_specs=[pl.BlockSpec((blk, proj), lambda b, i: (b * nblk + i, 0)),
                  pl.BlockSpec((None, 1, proj), lambda b, i: (b, 0, 0)),
                  pl.BlockSpec((None, heads, HEAD_DIM, HEAD_DIM), lambda b, i: (b, 0, 0, 0)),
                  const(vecs.shape), const(w2.shape), const(a2.shape), const(g2.shape),
                  const((rw, rw)), const(tri.shape)],
        out_specs=[pl.BlockSpec((blk, rw), lambda b, i: (b * nblk + i, 0)),
                   pl.BlockSpec((None, 1, proj), lambda b, i: (b, 0, 0)),
                   pl.BlockSpec((None, heads, HEAD_DIM, HEAD_DIM), lambda b, i: (b, 0, 0, 0))],
        out_shape=[jax.ShapeDtypeStruct((rows, rw), f32),
                   jax.ShapeDtypeStruct((batch, 1, proj), f32),
                   jax.ShapeDtypeStruct(state0.shape, f32)],
        scratch_shapes=[pltpu.VMEM((rw, rw), f32), pltpu.VMEM((1, proj), f32)]
                       + [pltpu.VMEM((blk, rw), f32)] * 7
                       + [pltpu.VMEM((2 * blk, rw), bf16), pltpu.VMEM((blk, rw), bf16), pltpu.VMEM((2 * blk, rw), f32),
                          pltpu.VMEM((blk, rw), bf16), pltpu.VMEM((2 * blk, rw), bf16), pltpu.VMEM((nchunk, rw), f32)],
        compiler_params=_params("arbitrary", "arbitrary"),
        name="rwkv_prompt",
    )(p_rw, shift_buf.reshape(batch, 1, proj), state0, vecs, w2, a2, g2, _seg_ones(rw), tri)


def _rwkv_sample_body(p_ref, shift_ref, s0_ref, vec_ref, w2_ref, a2_ref, g2_ref, seg_ref,
                      y_ref, nshift_ref, st_ref,
                      r_scr, k_scr, v_scr, kk_scr, kka_scr, dec_scr, o_scr):
    steps, nb, proj = p_ref.shape
    rw = y_ref.shape[-1]
    heads = rw // HEAD_DIM
    p = p_ref[...].reshape(steps * nb, proj)
    prev = jnp.concatenate([shift_ref[...], p[:(steps - 1) * nb]], axis=0)
    nshift_ref[...] = p[(steps - 1) * nb:]
    seg = seg_ref[...]
    r, k, v, kk, a, lw, g = _rwkv_prep(p, prev, vec_ref, w2_ref, a2_ref, g2_ref, seg)
    r_scr[...] = r
    k_scr[...] = k
    v_scr[...] = v
    kk_scr[...] = kk
    kka_scr[...] = kk * a
    dec_scr[...] = jnp.exp(lw)
    n = HEAD_DIM
    hs = range(heads)
    sl = [slice(h * n, (h + 1) * n) for h in hs]

    def one_sequence(i, carry):
        eye = (lax.broadcasted_iota(jnp.int32, (n, n), 0) == lax.broadcasted_iota(jnp.int32, (n, n), 1)).astype(f32)
        s = [s0_ref[i, h] for h in hs]
        for t in range(steps):
            at = pl.ds(t * nb + i, 1)
            r_t, k_t, v_t, kk_t, kka_t, dec_t = (scr[at, :] for scr in (r_scr, k_scr, v_scr, kk_scr, kka_scr, dec_scr))
            sa = [-jnp.sum(s[h] * kk_t[:, sl[h]], axis=1, keepdims=True) for h in hs]
            v_col = [jnp.sum(eye * v_t[:, sl[h]], axis=1, keepdims=True) for h in hs]
            s = [s[h] * dec_t[:, sl[h]] + sa[h] * kka_t[:, sl[h]] + v_col[h] * k_t[:, sl[h]] for h in hs]
            o_col = [jnp.sum(s[h] * r_t[:, sl[h]], axis=1, keepdims=True) for h in hs]
            o_scr[at, :] = jnp.concatenate([jnp.sum(eye * o_col[h], axis=0, keepdims=True) for h in hs], axis=1)
        for h in hs:
            st_ref[i, h] = s[h]
        return carry

    lax.fori_loop(0, nb, one_sequence, 0)
    y_ref[...] = _rwkv_finish(o_scr[...], r, k, v, g, vec_ref, seg).reshape(steps, nb, rw)


def rwkv_sample(p_rw, shift_buf, state0, vecs, w2, a2, g2, steps):
    rows, proj = p_rw.shape
    nseq = rows // steps
    rw = w2.shape[1]
    heads = rw // HEAD_DIM
    nb = SAMPLE_GROUP
    assert nseq % nb == 0
    const = lambda shape: pl.BlockSpec(shape, lambda j: (0,) * len(shape))
    y, nshift, st = pl.pallas_call(
        _rwkv_sample_body,
        grid=(nseq // nb,),
        in_specs=[pl.BlockSpec((steps, nb, proj), lambda j: (0, j, 0)),
                  pl.BlockSpec((nb, proj), lambda j: (j, 0)),
                  pl.BlockSpec((nb, heads, HEAD_DIM, HEAD_DIM), lambda j: (j, 0, 0, 0)),
                  const(vecs.shape), const(w2.shape), const(a2.shape), const(g2.shape), const((rw, rw))],
        out_specs=[pl.BlockSpec((steps, nb, rw), lambda j: (0, j, 0)),
                   pl.BlockSpec((nb, proj), lambda j: (j, 0)),
                   pl.BlockSpec((nb, heads, HEAD_DIM, HEAD_DIM), lambda j: (j, 0, 0, 0))],
        out_shape=[jax.ShapeDtypeStruct((steps, nseq, rw), f32),
                   jax.ShapeDtypeStruct((nseq, proj), f32),
                   jax.ShapeDtypeStruct(state0.shape, f32)],
        scratch_shapes=[pltpu.VMEM((steps * nb, rw), f32)] * 7,
        compiler_params=_params("arbitrary"),
        name="rwkv_sample",
    )(p_rw.reshape(steps, nseq, proj), shift_buf, state0, vecs, w2, a2, g2, _seg_ones(rw))
    return y.reshape(rows, rw), nshift, st


def _later_matrix(n, terms):
    idx = jnp.arange(n)
    m_later = (idx[:, None] > idx[None, :]).astype(bf16)
    return jnp.concatenate([m_later] * terms, axis=0)


def _sb_prompt_body(bias_ref, q_ref, k_ref, v_ref, m_ref, o_ref):
    i = pl.program_id(1)
    tq, width = q_ref.shape
    heads = width // HEAD_DIM
    tk = tq
    terms = m_ref.shape[0] // tk
    m_later = m_ref[...]
    hs = range(heads)
    sl = [slice(h * HEAD_DIM, (h + 1) * HEAD_DIM) for h in hs]
    qn = [(q_ref[:, sl[h]] * (-SB_SCALE * LOG2_E)).astype(bf16) for h in hs]

    def tile(kj, carries, diagonal):
        at = pl.ds(pl.multiple_of(kj * tk, tk), tk)
        if diagonal:
            below = lax.broadcasted_iota(jnp.int32, (tq, tk), 1) < lax.broadcasted_iota(jnp.int32, (tq, tk), 0)
        n2 = [_dot_nt(qn[h], k_ref[at, sl[h]]) - bias_ref[h] for h in hs]
        l1m = [_log2_one_minus_beta(n2[h]) for h in hs]
        log_beta = [l1m[h] - n2[h] for h in hs]
        if diagonal:
            l1m = [jnp.where(below, l1m[h], 0.0) for h in hs]
        parts = [jnp.concatenate(_split(l1m[h], terms), axis=1) for h in hs]
        suffix = [jnp.dot(parts[h], m_later, preferred_element_type=f32) for h in hs]
        w = [jnp.exp2(log_beta[h] + suffix[h] + carries[h]) for h in hs]
        if diagonal:
            w = [jnp.where(below, w[h], 0.0) for h in hs]
        out = [_bdot(w[h], v_ref[at, sl[h]]) for h in hs]
        for h in hs:
            if diagonal:
                o_ref[:, sl[h]] = out[h]
            else:
                o_ref[:, sl[h]] += out[h]
        return tuple(carries[h] + jnp.sum(l1m[h], axis=1, keepdims=True) for h in hs)

    carries = tile(i, tuple(jnp.zeros((tq, 1), f32) for _ in hs), True)
    lax.fori_loop(0, i, lambda t, c: tile(i - 1 - t, c, False), carries)


def sb_prompt(q, k, v, bias, batch):
    rows, width = q.shape
    t = rows // batch
    tq = min(SB_TILE, t)
    assert t % tq == 0
    nq = t // tq
    m_later = _later_matrix(tq, SB_PROMPT_SPLIT_TERMS)
    return pl.pallas_call(
        _sb_prompt_body,
        grid=(batch, nq),
        in_specs=[pl.BlockSpec(memory_space=pltpu.SMEM),
                  pl.BlockSpec((tq, width), lambda b, i: (b * nq + i, 0)),
                  pl.BlockSpec((t, width), lambda b, i: (b, 0)),
                  pl.BlockSpec((t, width), lambda b, i: (b, 0)),
                  pl.BlockSpec(m_later.shape, lambda b, i: (0, 0))],
        out_specs=pl.BlockSpec((tq, width), lambda b, i: (b * nq + i, 0)),
        out_shape=jax.ShapeDtypeStruct((rows, width), f32),
        compiler_params=_params("arbitrary", "arbitrary"),
        name="sb_prompt",
    )(bias.astype(f32) * LOG2_E, q, k, v, m_later)


def _sb_sample_body(pt_ref, q_ref, kn_ref, vn_ref, bias_ref, m_ref, ck_ref, cv_ref, o_ref, kbuf, vbuf, sem,
                    *, layer, pages, steps):
    b = pl.program_id(0)
    nseq = pl.num_programs(0)
    width = q_ref.shape[-1]
    heads = width // HEAD_DIM
    page = m_ref.shape[1]
    terms = m_ref.shape[0] // page
    rows = heads * SUBLANES
    hsl = lambda h: slice(h * HEAD_DIM, (h + 1) * HEAD_DIM)

    def page_copies(seq, slot):
        out = []
        for j in range(pages):
            pg = pt_ref[seq, j]
            at = pl.ds(j * page, page)
            for h in range(heads):
                out.append(pltpu.make_async_copy(ck_ref.at[layer, pg, :, h, :], kbuf.at[slot, h, at, :], sem.at[slot]))
                out.append(pltpu.make_async_copy(cv_ref.at[layer, pg, :, h, :], vbuf.at[slot, h, at, :], sem.at[slot]))
        return out

    slot = b % 2

    @pl.when(b == 0)
    def _():
        for cp in page_copies(0, 0):
            cp.start()

    @pl.when(b + 1 < nseq)
    def _():
        for cp in page_copies(b + 1, 1 - slot):
            cp.start()

    qn = q_ref[...] * (-SB_SCALE * LOG2_E)
    fill = jnp.zeros((page - SUBLANES, width), f32)
    kn = jnp.concatenate([kn_ref[...], fill], axis=0).astype(bf16)
    vn = jnp.concatenate([vn_ref[...], fill], axis=0).astype(bf16)

    for cp in page_copies(b, slot):
        cp.wait()

    n2 = jnp.concatenate(
        [_dot_nt(qn[:, hsl(h)], jnp.concatenate([kbuf[slot, h].astype(bf16), kn[:, hsl(h)]], axis=0))
         for h in range(heads)], axis=0) - bias_ref[...]
    l1m = _log2_one_minus_beta(n2)
    log_beta = l1m - n2
    seen = (lax.broadcasted_iota(jnp.int32, (rows, page), 1)
            < lax.broadcasted_iota(jnp.int32, (rows, page), 0) % SUBLANES)

    m_later = m_ref[...]
    carry = jnp.zeros((rows, 1), f32)
    w_parts = [None] * (pages + 1)
    for j in range(pages, -1, -1):
        at = slice(j * page, (j + 1) * page)
        lj = l1m[:, at]
        if j == pages:
            lj = jnp.where(seen, lj, 0.0)
        suffix = jnp.dot(jnp.concatenate(_split(lj, terms), axis=1), m_later, preferred_element_type=f32)
        w = jnp.exp2(log_beta[:, at] + suffix + carry)
        if j == pages:
            w = jnp.where(seen, w, 0.0)
        w_parts[j] = w.astype(bf16)
        carry = carry + jnp.sum(lj, axis=1, keepdims=True)
    w_all = jnp.concatenate(w_parts, axis=1)
    outs = [jnp.dot(w_all[h * SUBLANES:(h + 1) * SUBLANES],
                    jnp.concatenate([vbuf[slot, h].astype(bf16), vn[:, hsl(h)]], axis=0), preferred_element_type=f32)
            for h in range(heads)]
    o_ref[...] = jnp.concatenate(outs, axis=1)[:steps]


def sb_sample(q, k_new, v_new, bias, cache_k, cache_v, page_table, layer):
    nseq, steps, width = q.shape
    heads = width // HEAD_DIM
    pages = page_table.shape[1]
    page = cache_k.shape[2]
    assert steps <= SUBLANES and cache_k.shape[3:] == (heads, HEAD_DIM)
    pad8 = lambda t: jnp.pad(t, ((0, 0), (0, SUBLANES - steps), (0, 0)))
    bias_rows = jnp.repeat(bias.astype(f32) * LOG2_E, SUBLANES).reshape(heads * SUBLANES, 1)
    m_later = _later_matrix(page, SB_SAMPLE_SPLIT_TERMS)
    row8 = pl.BlockSpec((None, SUBLANES, width), lambda b, pt: (b, 0, 0))
    grid_spec = pltpu.PrefetchScalarGridSpec(
        num_scalar_prefetch=1,
        grid=(nseq,),
        in_specs=[row8, row8, row8,
                  pl.BlockSpec(bias_rows.shape, lambda b, pt: (0, 0)),
                  pl.BlockSpec(m_later.shape, lambda b, pt: (0, 0)),
                  pl.BlockSpec(memory_space=pl.ANY), pl.BlockSpec(memory_space=pl.ANY)],
        out_specs=pl.BlockSpec((None, steps, width), lambda b, pt: (b, 0, 0)),
        scratch_shapes=[pltpu.VMEM((2, heads, pages * page, HEAD_DIM), f32),
                        pltpu.VMEM((2, heads, pages * page, HEAD_DIM), f32),
                        pltpu.SemaphoreType.DMA((2,))],
    )
    return pl.pallas_call(
        functools.partial(_sb_sample_body, layer=layer, pages=pages, steps=steps),
        grid_spec=grid_spec,
        out_shape=jax.ShapeDtypeStruct((nseq, steps, width), f32),
        compiler_params=_params("arbitrary"),
        name="sb_sample",
    )(page_table, pad8(q), pad8(k_new), pad8(v_new), bias_rows, m_later, cache_k, cache_v)


def _out_ffn_body(x_ref, ysc_ref, yrw_ref, ysb_ref, wout_ref, wup_ref, wdown_ref, gains_ref,
                  o_ref, x1_scr, f_scr, acc_scr):
    j = pl.program_id(1)

    @pl.when(j == 0)
    def _():
        off, mix = 0, None
        for y_ref in (ysc_ref, yrw_ref, ysb_ref):
            n = y_ref.shape[-1]
            part = jnp.dot(y_ref[...].astype(bf16), wout_ref[off:off + n, :], preferred_element_type=f32)
            mix = part if mix is None else mix + part
            off += n
        x1 = x_ref[...] + _rms_rows(mix, gains_ref[0:1, :])
        x1_scr[...] = x1
        f_scr[...] = _rms_rows(x1, gains_ref[1:2, :]).astype(bf16)
        acc_scr[...] = jnp.zeros_like(acc_scr)

    hid = jnp.maximum(jnp.dot(f_scr[...], wup_ref[...], preferred_element_type=f32), 0.0)
    acc_scr[...] += jnp.dot((hid * hid).astype(bf16), wdown_ref[...], preferred_element_type=f32)

    @pl.when(j == pl.num_programs(1) - 1)
    def _():
        o_ref[...] = x1_scr[...] + _rms_rows(acc_scr[...], gains_ref[2:3, :])


def out_ffn(x, y_sc, y_rw, y_sb, w_out_bf, w_up_bf, w_down_bf, gains):
    rows, d = x.shape
    dff = w_up_bf.shape[1]
    tm = min(ROW_TILE, rows)
    tf = min(FF_TILE, dff)
    assert rows % tm == 0 and dff % tf == 0
    row_spec = lambda n: pl.BlockSpec((tm, n), lambda i, j: (i, 0))
    return pl.pallas_call(
        _out_ffn_body,
        grid=(rows // tm, dff // tf),
        in_specs=[row_spec(d), row_spec(y_sc.shape[1]), row_spec(y_rw.shape[1]), row_spec(y_sb.shape[1]),
                  pl.BlockSpec((d, d), lambda i, j: (0, 0)),
                  pl.BlockSpec((d, tf), lambda i, j: (0, j)),
                  pl.BlockSpec((tf, d), lambda i, j: (j, 0)),
                  pl.BlockSpec(gains.shape, lambda i, j: (0, 0))],
        out_specs=row_spec(d),
        out_shape=jax.ShapeDtypeStruct((rows, d), f32),
        scratch_shapes=[pltpu.VMEM((tm, d), f32), pltpu.VMEM((tm, d), bf16), pltpu.VMEM((tm, d), f32)],
        compiler_params=_params("arbitrary", "arbitrary"),
        name="out_ffn",
    )(x, y_sc, y_rw, y_sb, w_out_bf, w_up_bf, w_down_bf, gains)


def kernel(x_prompt, x_sample, state_conv, state_shift, state_rwkv, cache_k, cache_v, page_table, w_in, conv_w, mu_shift, w0, w2, a0, a2, g2, k_k, k_a, r_k, ln_w, ln_b, sb_bias, w_out, w_up, w_down, g_pre_mix, g_post_mix, g_pre_ffn, g_post_ffn):
    depth = w_in.shape[0]
    bp, tp, d = x_prompt.shape
    bs, ts, _ = x_sample.shape
    sc_w = conv_w.shape[-1]
    rw_w = w0.shape[-1]
    rw_proj = mu_shift.shape[-1]
    sb_heads = sb_bias.shape[-1]
    sb_w = sb_heads * HEAD_DIM
    rw_heads = rw_w // HEAD_DIM
    widths = (3 * sc_w, rw_proj, sb_w, sb_w, sb_w)
    assert sum(widths) == w_in.shape[-1]

    w_in_bf, w_out_bf, w_up_bf, w_down_bf = (w.astype(bf16) for w in (w_in, w_out, w_up, w_down))

    xp = x_prompt.reshape(bp * tp, d)
    xs = jnp.swapaxes(x_sample, 0, 1).reshape(ts * bs, d)
    conv0 = jnp.zeros((bp, 2, sc_w), f32)
    shift0 = jnp.zeros((bp, rw_proj), f32)
    rw0 = jnp.zeros((bp, rw_heads, HEAD_DIM, HEAD_DIM), f32)

    outs_p, outs_s = [], []
    for l in range(depth):
        vecs = _rwkv_vectors(mu_shift[l], w0[l], a0[l], k_k[l], k_a[l], r_k[l], ln_w[l], ln_b[l])
        gains = jnp.stack([g_post_mix[l], g_pre_ffn[l], g_post_ffn[l]])
        lora = (w2[l], a2[l], g2[l])

        p_sc, p_rw, q, k, v = in_proj(xp, g_pre_mix[l], w_in_bf[l], widths)
        y_sc, n_conv = sconv_prompt(p_sc, conv0, conv_w[l], bp)
        y_rw, n_shift, n_state = rwkv_prompt(p_rw, shift0, rw0, vecs, *lora, bp)
        y_sb = sb_prompt(q, k, v, sb_bias[l], bp)
        xp = out_ffn(xp, y_sc, y_rw, y_sb, w_out_bf[l], w_up_bf[l], w_down_bf[l], gains)
        outs_p.append((n_conv, n_shift.reshape(bp, rw_proj), n_state,
                       k.reshape(bp, tp, sb_heads, HEAD_DIM), v.reshape(bp, tp, sb_heads, HEAD_DIM)))

        p_sc, p_rw, q, k, v = in_proj(xs, g_pre_mix[l], w_in_bf[l], widths)
        y_sc, n_conv = sconv_sample(p_sc, jnp.swapaxes(state_conv[l], 0, 1), conv_w[l], ts)
        y_rw, n_shift, n_state = rwkv_sample(p_rw, state_shift[l], state_rwkv[l], vecs, *lora, ts)
        to_bm = lambda t: jnp.swapaxes(t.reshape(ts, bs, sb_w), 0, 1)
        y_sb = sb_sample(to_bm(q), to_bm(k), to_bm(v), sb_bias[l], cache_k, cache_v, page_table, l)
        y_sb = jnp.swapaxes(y_sb, 0, 1).reshape(ts * bs, sb_w)
        xs = out_ffn(xs, y_sc, y_rw, y_sb, w_out_bf[l], w_up_bf[l], w_down_bf[l], gains)
        outs_s.append((jnp.swapaxes(n_conv, 0, 1), n_shift, n_state,
                       to_bm(k).reshape(bs, ts, sb_heads, HEAD_DIM), to_bm(v).reshape(bs, ts, sb_heads, HEAD_DIM)))

    p_conv, p_shift, p_rwkv, p_k, p_v = (jnp.stack(t) for t in zip(*outs_p))
    s_conv, s_shift, s_rwkv, s_k, s_v = (jnp.stack(t) for t in zip(*outs_s))
    y_prompt = xp.reshape(bp, tp, d)
    y_sample = jnp.swapaxes(xs.reshape(ts, bs, d), 0, 1)
    return (y_prompt, y_sample, p_conv, p_shift, p_rwkv, p_k, p_v, s_conv, s_shift, s_rwkv, s_k, s_v)
```

```python
import functools

import jax
import jax.numpy as jnp
from jax import lax
from jax.experimental import pallas as pl
from jax.experimental.pallas import tpu as pltpu

f32 = jnp.float32
bf16 = jnp.bfloat16

HEAD_DIM = 64
NORM_EPS = 1e-6
RW_GN_EPS = HEAD_DIM * 1e-5
SB_SCALE = HEAD_DIM ** -0.5
LOG2_E = 1.4426950408889634
V7X_VMEM_LIMIT_BYTES = 56 * 1024 * 1024
V7X_MXU_LANES = 256
ROW_TILE = 512
FF_TILE = 1024
RW_CHUNK = 64
RW_BLOCK = 512
RW_PREP_GROUP = 8
SB_PROMPT_SPLIT_TERMS = 1
SB_SAMPLE_SPLIT_TERMS = 2
SB_TILE = 256
SAMPLE_GROUP = 8
SUBLANES = 8


def _params(*sem):
    return pltpu.CompilerParams(dimension_semantics=sem or None, vmem_limit_bytes=V7X_VMEM_LIMIT_BYTES)


def _bdot(a, b):
    return jnp.dot(a.astype(bf16), b.astype(bf16), preferred_element_type=f32)


def _dot_nt(a, b):
    return lax.dot_general(a.astype(bf16), b.astype(bf16), (((1,), (1,)), ((), ())), preferred_element_type=f32)


def _dot_tn(a, b):
    return lax.dot_general(a.astype(bf16), b.astype(bf16), (((0,), (0,)), ((), ())), preferred_element_type=f32)


def _split(x, terms):
    parts = []
    for _ in range(terms):
        p = x.astype(bf16)
        parts.append(p)
        x = x - p.astype(f32)
    return parts


def _dot_x01(x, m01, terms=2):
    return sum(jnp.dot(p, m01, preferred_element_type=f32) for p in _split(x, terms))


def _dot_01x(m01, x, terms=2):
    return sum(jnp.dot(m01, p, preferred_element_type=f32) for p in _split(x, terms))


def _rms_rows(x, g):
    return x * lax.rsqrt(jnp.mean(x * x, axis=-1, keepdims=True) + NORM_EPS) * g


def _softplus(x):
    return jnp.maximum(x, 0.0) + jnp.log(1.0 + jnp.exp(-jnp.abs(x)))


def _log2_one_minus_beta(n2):
    return jnp.minimum(n2, 0.0) - jnp.log2(1.0 + jnp.exp2(-jnp.abs(n2)))


def _sigmoid(x):
    return 1.0 / (1.0 + jnp.exp(-x))


def _lane_groups(width):
    return [(g, min(g + V7X_MXU_LANES, width)) for g in range(0, width, V7X_MXU_LANES)]


def _head_blocks(x, heads):
    c = x.shape[0]
    t = jnp.concatenate([x] * heads, axis=0)
    keep = (lax.broadcasted_iota(jnp.int32, t.shape, 0) // c) == (lax.broadcasted_iota(jnp.int32, t.shape, 1) // HEAD_DIM)
    return jnp.where(keep, t, jnp.zeros_like(t))


def _bd_dot(lhs, x):
    outs = []
    for g0, g1 in _lane_groups(x.shape[1]):
        xb = _head_blocks(x[:, g0:g1].astype(bf16), (g1 - g0) // HEAD_DIM)
        outs.append(jnp.dot(lhs[:, g0:g1].astype(bf16), xb, preferred_element_type=f32))
    return jnp.concatenate(outs, axis=1)


def _bd_dot_nt(lhs, x):
    outs = []
    for g0, g1 in _lane_groups(x.shape[1]):
        xb = _head_blocks(x[:, g0:g1].astype(bf16), (g1 - g0) // HEAD_DIM)
        outs.append(_dot_nt(lhs[:, g0:g1], xb))
    return jnp.concatenate(outs, axis=1)


def _in_proj_body(x_ref, g_ref, w_ref, *out_refs):
    h = _rms_rows(x_ref[...], g_ref[...]).astype(bf16)
    off = 0
    for o_ref in out_refs:
        n = o_ref.shape[-1]
        o_ref[...] = jnp.dot(h, w_ref[:, off:off + n], preferred_element_type=f32)
        off += n


def in_proj(x, g, w_bf, widths):
    rows, d = x.shape
    tm = min(ROW_TILE, rows)
    assert rows % tm == 0 and sum(widths) == w_bf.shape[1]
    return pl.pallas_call(
        _in_proj_body,
        grid=(rows // tm,),
        in_specs=[pl.BlockSpec((tm, d), lambda i: (i, 0)),
                  pl.BlockSpec((1, d), lambda i: (0, 0)),
                  pl.BlockSpec(w_bf.shape, lambda i: (0, 0))],
        out_specs=[pl.BlockSpec((tm, n), lambda i: (i, 0)) for n in widths],
        out_shape=[jax.ShapeDtypeStruct((rows, n), f32) for n in widths],
        compiler_params=_params("arbitrary"),
        name="in_proj",
    )(x, g.reshape(1, d), w_bf)


def _sconv_seq_body(p_ref, buf_ref, w_ref, y_ref, nb_ref):
    t, w = y_ref.shape
    gate = p_ref[:, 0:w]
    u = p_ref[:, w:2 * w] * p_ref[:, 2 * w:3 * w]
    row = lax.broadcasted_iota(jnp.int32, (t, w), 0)
    buf = buf_ref[...]
    um1 = jnp.where(row == 0, buf[1:2], pltpu.roll(u, 1, 0))
    um2 = jnp.where(row == 0, buf[0:1], jnp.where(row == 1, buf[1:2], pltpu.roll(u, 2, 0)))
    cw = w_ref[...]
    y_ref[...] = gate * (cw[0:1] * um2 + cw[1:2] * um1 + cw[2:3] * u)
    nb_ref[...] = u[t - 2:t]


def sconv_prompt(p_sc, conv_buf, conv_w, batch):
    rows, w3 = p_sc.shape
    t, w = rows // batch, w3 // 3
    assert conv_w.shape[0] == 3 and t >= 2
    return pl.pallas_call(
        _sconv_seq_body,
        grid=(batch,),
        in_specs=[pl.BlockSpec((t, w3), lambda b: (b, 0)),
                  pl.BlockSpec((None, 2, w), lambda b: (b, 0, 0)),
                  pl.BlockSpec((3, w), lambda b: (0, 0))],
        out_specs=[pl.BlockSpec((t, w), lambda b: (b, 0)),
                   pl.BlockSpec((None, 2, w), lambda b: (b, 0, 0))],
        out_shape=[jax.ShapeDtypeStruct((rows, w), f32), jax.ShapeDtypeStruct((batch, 2, w), f32)],
        compiler_params=_params("arbitrary"),
        name="sconv_prompt",
    )(p_sc, conv_buf, conv_w)


def _sconv_tm_body(p_ref, buf_ref, w_ref, y_ref, nb_ref, *, steps):
    rows, w = y_ref.shape
    nb = rows // steps
    gate = p_ref[:, 0:w]
    u = p_ref[:, w:2 * w] * p_ref[:, 2 * w:3 * w]
    pad = [buf_ref[0], buf_ref[1]] + [u[t * nb:(t + 1) * nb] for t in range(steps)]
    cw = w_ref[...]
    for t in range(steps):
        y_ref[t * nb:(t + 1) * nb, :] = gate[t * nb:(t + 1) * nb] * (
            cw[0:1] * pad[t] + cw[1:2] * pad[t + 1] + cw[2:3] * pad[t + 2])
    nb_ref[0] = pad[steps]
    nb_ref[1] = pad[steps + 1]


def sconv_sample(p_sc, conv_buf_tm, conv_w, steps):
    rows, w3 = p_sc.shape
    w = w3 // 3
    return pl.pallas_call(
        functools.partial(_sconv_tm_body, steps=steps),
        out_shape=[jax.ShapeDtypeStruct((rows, w), f32), jax.ShapeDtypeStruct(conv_buf_tm.shape, f32)],
        compiler_params=_params(),
        name="sconv_sample",
    )(p_sc, conv_buf_tm, conv_w)


def _rwkv_prep(p, prev, vec_ref, w2_ref, a2_ref, g2_ref, seg):
    rw = seg.shape[0]
    dl, al = w2_ref.shape[0], a2_ref.shape[0]
    mu = vec_ref[0:1, :]
    w0, a0, kkw, kaw = (vec_ref[i:i + 1, 0:rw] for i in (1, 2, 3, 4))
    xs = p + (prev - p) * mu
    r, k, v = xs[:, 0:rw], xs[:, rw:2 * rw], xs[:, 2 * rw:3 * rw]
    o = 3 * rw
    xw, xa, xg = xs[:, o:o + dl], xs[:, o + dl:o + dl + al], xs[:, o + dl + al:]
    log_w = -_softplus(-(w0 + _bdot(jnp.tanh(xw), w2_ref[...]))) - 0.5
    lw = -jnp.exp(log_w)
    a = _sigmoid(a0 + _bdot(xa, a2_ref[...]))
    g = _bdot(_sigmoid(xg), g2_ref[...])
    kk = k * kkw
    k = k * (1.0 + (a - 1.0) * kaw)
    kk = kk * jnp.minimum(lax.rsqrt(_dot_x01(kk * kk, seg, terms=1)), 1e12)
    return r, k, v, kk, a, lw, g


def _rwkv_finish(o, r, k, v, g, vec_ref, seg):
    rw = seg.shape[0]
    rk, lnw, lnb = (vec_ref[i:i + 1, 0:rw] for i in (5, 6, 7))
    inv_n = 1.0 / HEAD_DIM
    mean = _dot_x01(o, seg) * inv_n
    d = o - mean
    var = _dot_x01(d * d, seg, terms=1) * inv_n
    on = d * lax.rsqrt(var + RW_GN_EPS) * lnw + lnb
    bonus = _dot_x01(r * k * rk, seg, terms=1) * v
    return (on + bonus) * g


def _rwkv_vectors(mu, w0, a0, k_k, k_a, r_k, ln_w, ln_b):
    proj = mu.shape[-1]
    rows = [mu] + [jnp.pad(t.reshape(-1), (0, proj - t.size)) for t in (w0, a0, k_k, k_a, r_k, ln_w, ln_b)]
    return jnp.stack(rows).astype(f32)


def _seg_ones(width):
    i = jnp.arange(width) // HEAD_DIM
    return (i[:, None] == i[None, :]).astype(bf16)


def _rwkv_prompt_body(p_ref, shift_ref, s0_ref, vec_ref, w2_ref, a2_ref, g2_ref, seg_ref, tri_ref,
                      y_ref, nshift_ref, st_ref,
                      s_scr, last_scr, r_scr, k_scr, v_scr, kk_scr, a_scr, lw_scr, o_scr,
                      ar_scr, tm_scr, lv_scr, arb_scr, bkg_scr, g_scr):
    tb = pl.program_id(1)
    rows, rw = y_ref.shape
    heads = rw // HEAD_DIM
    c = RW_CHUNK
    nchunk = rows // c
    hsl = lambda h: slice(h * HEAD_DIM, (h + 1) * HEAD_DIM)

    @pl.when(tb == 0)
    def _():
        s_scr[...] = jnp.zeros_like(s_scr)
        for h in range(heads):
            s_scr[hsl(h), hsl(h)] = s0_ref[h]
        last_scr[...] = shift_ref[...]

    p = p_ref[...]
    row = lax.broadcasted_iota(jnp.int32, p.shape, 0)
    prev = jnp.where(row == 0, last_scr[...], pltpu.roll(p, 1, 0))
    last_scr[...] = p[rows - 1:rows]
    nshift_ref[...] = p[rows - 1:rows]
    seg = seg_ref[...]
    r, k, v, kk, a, lw, g = _rwkv_prep(p, prev, vec_ref, w2_ref, a2_ref, g2_ref, seg)
    r_scr[...] = r
    k_scr[...] = k
    v_scr[...] = v
    kk_scr[...] = kk
    a_scr[...] = a
    lw_scr[...] = lw

    group = RW_PREP_GROUP if nchunk % RW_PREP_GROUP == 0 else 1

    def prepare(n, carry):
        ri = lax.broadcasted_iota(jnp.int32, (2 * c, rw), 0)
        cj = lax.broadcasted_iota(jnp.int32, (2 * c, rw), 1) % HEAD_DIM
        tri_mask = cj < jnp.where(ri < c, ri, ri - c + 1)
        eye = (lax.broadcasted_iota(jnp.int32, (c, rw), 0)
               == lax.broadcasted_iota(jnp.int32, (c, rw), 1) % HEAD_DIM).astype(f32)
        gs = range(group)
        sl_t = [pl.ds(pl.multiple_of((n * group + i) * c, c), c) for i in gs]
        sl_2t = [pl.ds(pl.multiple_of((n * group + i) * 2 * c, 2 * c), 2 * c) for i in gs]
        ld = lambda scr: [scr[sl_t[i], :] for i in gs]
        rc, kc, vc, kkc, ac, lwc = (ld(s) for s in (r_scr, k_scr, v_scr, kk_scr, a_scr, lw_scr))
        cs = [_dot_01x(tri_ref[...], lwc[i], terms=3) for i in gs]
        tot = [cs[i][c - 1:c, :] for i in gs]
        e_neg = [jnp.exp(-cs[i]) for i in gs]
        e_rem = [jnp.exp(tot[i] - cs[i]) for i in gs]
        kka = [kkc[i] * ac[i] for i in gs]
        ar = [jnp.concatenate([-kkc[i] * jnp.exp(cs[i] - lwc[i]), rc[i] * jnp.exp(cs[i])], axis=0).astype(bf16)
              for i in gs]
        b_t = [(kka[i] * e_neg[i]).astype(bf16) for i in gs]
        k_t = [(kc[i] * e_neg[i]).astype(bf16) for i in gs]
        m_b = [jnp.where(tri_mask, _bd_dot_nt(ar[i], b_t[i]), 0.0) for i in gs]
        m_k = [jnp.where(tri_mask, _bd_dot_nt(ar[i], k_t[i]), 0.0) for i in gs]
        pw = [m_b[i][:c] for i in gs]
        tm = [eye + pw[i] for i in gs]
        span = 1
        while 2 * span < c:
            pw = [_bd_dot(pw[i], pw[i]) for i in gs]
            tm = [tm[i] + _bd_dot(tm[i], pw[i]) for i in gs]
            span *= 2
        lv = [_bd_dot(m_k[i], vc[i]) for i in gs]
        for i in gs:
            ar_scr[sl_2t[i], :] = ar[i]
            tm_scr[sl_t[i], :] = tm[i].astype(bf16)
            lv_scr[sl_2t[i], :] = lv[i]
            arb_scr[sl_t[i], :] = m_b[i][c:].astype(bf16)
            bkg_scr[sl_2t[i], :] = jnp.concatenate([kka[i] * e_rem[i], kc[i] * e_rem[i]], axis=0).astype(bf16)
            g_scr[pl.ds(n * group + i, 1), :] = jnp.exp(tot[i])
        return carry

    lax.fori_loop(0, nchunk // group, prepare, 0)

    def advance(n, carry):
        sl_t = pl.ds(pl.multiple_of(n * c, c), c)
        sl_2t = pl.ds(pl.multiple_of(n * 2 * c, 2 * c), 2 * c)
        ar = ar_scr[sl_2t, :]
        x = jnp.concatenate([_dot_nt(ar[:, g0:g1], s_scr[g0:g1, g0:g1]) for g0, g1 in _lane_groups(rw)], axis=1)
        x = x + lv_scr[sl_2t, :]
        u = _bd_dot(tm_scr[sl_t, :], x[:c])
        o_scr[sl_t, :] = x[c:] + _bd_dot(arb_scr[sl_t, :], u)
        uv = jnp.concatenate([u.astype(bf16), v_scr[sl_t, :].astype(bf16)], axis=0)
        bkg = bkg_scr[sl_2t, :]
        gdec = g_scr[pl.ds(n, 1), :]
        for g0, g1 in _lane_groups(rw):
            upd = _dot_tn(uv[:, g0:g1], bkg[:, g0:g1])
            same_head = (lax.broadcasted_iota(jnp.int32, upd.shape, 0) // HEAD_DIM
                         == lax.broadcasted_iota(jnp.int32, upd.shape, 1) // HEAD_DIM)
            s_scr[g0:g1, g0:g1] = s_scr[g0:g1, g0:g1] * gdec[:, g0:g1] + jnp.where(same_head, upd, 0.0)
        return carry

    lax.fori_loop(0, nchunk, advance, 0)
    y_ref[...] = _rwkv_finish(o_scr[...], r, k, v, g, vec_ref, seg)

    @pl.when(tb == pl.num_programs(1) - 1)
    def _():
        for h in range(heads):
            st_ref[h] = s_scr[hsl(h), hsl(h)]


def rwkv_prompt(p_rw, shift_buf, state0, vecs, w2, a2, g2, batch):
    rows, proj = p_rw.shape
    t = rows // batch
    rw = w2.shape[1]
    heads = rw // HEAD_DIM
    blk = min(RW_BLOCK, t)
    assert t % blk == 0 and blk % RW_CHUNK == 0 and RW_CHUNK == HEAD_DIM
    nblk = t // blk
    nchunk = blk // RW_CHUNK
    tri = jnp.tril(jnp.ones((RW_CHUNK, RW_CHUNK), bf16))
    const = lambda shape: pl.BlockSpec(shape, lambda b, i: (0,) * len(shape))
    return pl.pallas_call(
        _rwkv_prompt_body,
        grid=(batch, nblk),
        in_specs=[pl.BlockSpec((blk, proj), lambda b, i: (b * nblk + i, 0)),
                  pl.BlockSpec((None, 1, proj), lambda b, i: (b, 0, 0)),
                  pl.BlockSpec((None, heads, HEAD_DIM, HEAD_DIM), lambda b, i: (b, 0, 0, 0)),
                  const(vecs.shape), const(w2.shape), const(a2.shape), const(g2.shape),
                  const((rw, rw)), const(tri.shape)],
        out_specs=[pl.BlockSpec((blk, rw), lambda b, i: (b * nblk + i, 0)),
                   pl.BlockSpec((None, 1, proj), lambda b, i: (b, 0, 0)),
                   pl.BlockSpec((None, heads, HEAD_DIM, HEAD_DIM), lambda b, i: (b, 0, 0, 0))],
        out_shape=[jax.ShapeDtypeStruct((rows, rw), f32),
                   jax.ShapeDtypeStruct((batch, 1, proj), f32),
                   jax.ShapeDtypeStruct(state0.shape, f32)],
        scratch_shapes=[pltpu.VMEM((rw, rw), f32), pltpu.VMEM((1, proj), f32)]
                       + [pltpu.VMEM((blk, rw), f32)] * 7
                       + [pltpu.VMEM((2 * blk, rw), bf16), pltpu.VMEM((blk, rw), bf16), pltpu.VMEM((2 * blk, rw), f32),
                          pltpu.VMEM((blk, rw), bf16), pltpu.VMEM((2 * blk, rw), bf16), pltpu.VMEM((nchunk, rw), f32)],
        compiler_params=_params("arbitrary", "arbitrary"),
        name="rwkv_prompt",
    )(p_rw, shift_buf.reshape(batch, 1, proj), state0, vecs, w2, a2, g2, _seg_ones(rw), tri)


def _rwkv_sample_body(p_ref, shift_ref, s0_ref, vec_ref, w2_ref, a2_ref, g2_ref, seg_ref,
                      y_ref, nshift_ref, st_ref,
                      r_scr, k_scr, v_scr, kk_scr, kka_scr, dec_scr, o_scr):
    steps, nb, proj = p_ref.shape
    rw = y_ref.shape[-1]
    heads = rw // HEAD_DIM
    p = p_ref[...].reshape(steps * nb, proj)
    prev = jnp.concatenate([shift_ref[...], p[:(steps - 1) * nb]], axis=0)
    nshift_ref[...] = p[(steps - 1) * nb:]
    seg = seg_ref[...]
    r, k, v, kk, a, lw, g = _rwkv_prep(p, prev, vec_ref, w2_ref, a2_ref, g2_ref, seg)
    r_scr[...] = r
    k_scr[...] = k
    v_scr[...] = v
    kk_scr[...] = kk
    kka_scr[...] = kk * a
    dec_scr[...] = jnp.exp(lw)
    n = HEAD_DIM
    hs = range(heads)
    sl = [slice(h * n, (h + 1) * n) for h in hs]

    def one_sequence(i, carry):
        eye = (lax.broadcasted_iota(jnp.int32, (n, n), 0) == lax.broadcasted_iota(jnp.int32, (n, n), 1)).astype(f32)
        s = [s0_ref[i, h] for h in hs]
        for t in range(steps):
            at = pl.ds(t * nb + i, 1)
            r_t, k_t, v_t, kk_t, kka_t, dec_t = (scr[at, :] for scr in (r_scr, k_scr, v_scr, kk_scr, kka_scr, dec_scr))
            sa = [-jnp.sum(s[h] * kk_t[:, sl[h]], axis=1, keepdims=True) for h in hs]
            v_col = [jnp.sum(eye * v_t[:, sl[h]], axis=1, keepdims=True) for h in hs]
            s = [s[h] * dec_t[:, sl[h]] + sa[h] * kka_t[:, sl[h]] + v_col[h] * k_t[:, sl[h]] for h in hs]
            o_col = [jnp.sum(s[h] * r_t[:, sl[h]], axis=1, keepdims=True) for h in hs]
            o_scr[at, :] = jnp.concatenate([jnp.sum(eye * o_col[h], axis=0, keepdims=True) for h in hs], axis=1)
        for h in hs:
            st_ref[i, h] = s[h]
        return carry

    lax.fori_loop(0, nb, one_sequence, 0)
    y_ref[...] = _rwkv_finish(o_scr[...], r, k, v, g, vec_ref, seg).reshape(steps, nb, rw)


def rwkv_sample(p_rw, shift_buf, state0, vecs, w2, a2, g2, steps):
    rows, proj = p_rw.shape
    nseq = rows // steps
    rw = w2.shape[1]
    heads = rw // HEAD_DIM
    nb = SAMPLE_GROUP
    assert nseq % nb == 0
    const = lambda shape: pl.BlockSpec(shape, lambda j: (0,) * len(shape))
    y, nshift, st = pl.pallas_call(
        _rwkv_sample_body,
        grid=(nseq // nb,),
        in_specs=[pl.BlockSpec((steps, nb, proj), lambda j: (0, j, 0)),
                  pl.BlockSpec((nb, proj), lambda j: (j, 0)),
                  pl.BlockSpec((nb, heads, HEAD_DIM, HEAD_DIM), lambda j: (j, 0, 0, 0)),
                  const(vecs.shape), const(w2.shape), const(a2.shape), const(g2.shape), const((rw, rw))],
        out_specs=[pl.BlockSpec((steps, nb, rw), lambda j: (0, j, 0)),
                   pl.BlockSpec((nb, proj), lambda j: (j, 0)),
                   pl.BlockSpec((nb, heads, HEAD_DIM, HEAD_DIM), lambda j: (j, 0, 0, 0))],
        out_shape=[jax.ShapeDtypeStruct((steps, nseq, rw), f32),
                   jax.ShapeDtypeStruct((nseq, proj), f32),
                   jax.ShapeDtypeStruct(state0.shape, f32)],
        scratch_shapes=[pltpu.VMEM((steps * nb, rw), f32)] * 7,
        compiler_params=_params("arbitrary"),
        name="rwkv_sample",
    )(p_rw.reshape(steps, nseq, proj), shift_buf, state0, vecs, w2, a2, g2, _seg_ones(rw))
    return y.reshape(rows, rw), nshift, st


def _later_matrix(n, terms):
    idx = jnp.arange(n)
    m_later = (idx[:, None] > idx[None, :]).astype(bf16)
    return jnp.concatenate([m_later] * terms, axis=0)


def _sb_prompt_body(bias_ref, q_ref, k_ref, v_ref, m_ref, o_ref):
    i = pl.program_id(1)
    tq, width = q_ref.shape
    heads = width // HEAD_DIM
    tk = tq
    terms = m_ref.shape[0] // tk
    m_later = m_ref[...]
    hs = range(heads)
    sl = [slice(h * HEAD_DIM, (h + 1) * HEAD_DIM) for h in hs]
    qn = [(q_ref[:, sl[h]] * (-SB_SCALE * LOG2_E)).astype(bf16) for h in hs]

    def tile(kj, carries, diagonal):
        at = pl.ds(pl.multiple_of(kj * tk, tk), tk)
        if diagonal:
            below = lax.broadcasted_iota(jnp.int32, (tq, tk), 1) < lax.broadcasted_iota(jnp.int32, (tq, tk), 0)
        n2 = [_dot_nt(qn[h], k_ref[at, sl[h]]) - bias_ref[h] for h in hs]
        l1m = [_log2_one_minus_beta(n2[h]) for h in hs]
        log_beta = [l1m[h] - n2[h] for h in hs]
        if diagonal:
            l1m = [jnp.where(below, l1m[h], 0.0) for h in hs]
        parts = [jnp.concatenate(_split(l1m[h], terms), axis=1) for h in hs]
        suffix = [jnp.dot(parts[h], m_later, preferred_element_type=f32) for h in hs]
        w = [jnp.exp2(log_beta[h] + suffix[h] + carries[h]) for h in hs]
        if diagonal:
            w = [jnp.where(below, w[h], 0.0) for h in hs]
        out = [_bdot(w[h], v_ref[at, sl[h]]) for h in hs]
        for h in hs:
            if diagonal:
                o_ref[:, sl[h]] = out[h]
            else:
                o_ref[:, sl[h]] += out[h]
        return tuple(carries[h] + jnp.sum(l1m[h], axis=1, keepdims=True) for h in hs)

    carries = tile(i, tuple(jnp.zeros((tq, 1), f32) for _ in hs), True)
    lax.fori_loop(0, i, lambda t, c: tile(i - 1 - t, c, False), carries)


def sb_prompt(q, k, v, bias, batch):
    rows, width = q.shape
    t = rows // batch
    tq = min(SB_TILE, t)
    assert t % tq == 0
    nq = t // tq
    m_later = _later_matrix(tq, SB_PROMPT_SPLIT_TERMS)
    return pl.pallas_call(
        _sb_prompt_body,
        grid=(batch, nq),
        in_specs=[pl.BlockSpec(memory_space=pltpu.SMEM),
                  pl.BlockSpec((tq, width), lambda b, i: (b * nq + i, 0)),
                  pl.BlockSpec((t, width), lambda b, i: (b, 0)),
                  pl.BlockSpec((t, width), lambda b, i: (b, 0)),
                  pl.BlockSpec(m_later.shape, lambda b, i: (0, 0))],
        out_specs=pl.BlockSpec((tq, width), lambda b, i: (b * nq + i, 0)),
        out_shape=jax.ShapeDtypeStruct((rows, width), f32),
        compiler_params=_params("arbitrary", "arbitrary"),
        name="sb_prompt",
    )(bias.astype(f32) * LOG2_E, q, k, v, m_later)


def _sb_sample_body(pt_ref, q_ref, kn_ref, vn_ref, bias_ref, m_ref, *rest, pages, steps):
    k_refs, v_refs, o_ref = rest[:pages], rest[pages:2 * pages], rest[2 * pages]
    del pt_ref
    width = q_ref.shape[-1]
    heads = width // HEAD_DIM
    page = m_ref.shape[1]
    terms = m_ref.shape[0] // page
    rows = heads * SUBLANES
    hsl = lambda h: slice(h * HEAD_DIM, (h + 1) * HEAD_DIM)
    qn = (q_ref[...] * (-SB_SCALE * LOG2_E)).astype(bf16)
    fill = jnp.zeros((page - SUBLANES, width), f32)
    kn = jnp.concatenate([kn_ref[...], fill], axis=0).astype(bf16)
    vn = jnp.concatenate([vn_ref[...], fill], axis=0).astype(bf16)

    n2 = jnp.concatenate(
        [jnp.concatenate([jnp.dot(qn[:, hsl(h)], jnp.concatenate([r[h].astype(bf16) for r in k_refs], axis=1),
                                  preferred_element_type=f32),
                          _dot_nt(qn[:, hsl(h)], kn[:, hsl(h)])], axis=1)
         for h in range(heads)], axis=0) - bias_ref[...]
    l1m = _log2_one_minus_beta(n2)
    log_beta = l1m - n2
    seen = (lax.broadcasted_iota(jnp.int32, (rows, page), 1)
            < lax.broadcasted_iota(jnp.int32, (rows, page), 0) % SUBLANES)

    m_later = m_ref[...]
    carry = jnp.zeros((rows, 1), f32)
    w_parts = [None] * (pages + 1)
    for j in range(pages, -1, -1):
        at = slice(j * page, (j + 1) * page)
        lj = l1m[:, at]
        if j == pages:
            lj = jnp.where(seen, lj, 0.0)
        suffix = jnp.dot(jnp.concatenate(_split(lj, terms), axis=1), m_later, preferred_element_type=f32)
        w = jnp.exp2(log_beta[:, at] + suffix + carry)
        if j == pages:
            w = jnp.where(seen, w, 0.0)
        w_parts[j] = w.astype(bf16)
        carry = carry + jnp.sum(lj, axis=1, keepdims=True)
    w_past = jnp.concatenate(w_parts[:pages], axis=1)
    outs = []
    for h in range(heads):
        mine = slice(h * SUBLANES, (h + 1) * SUBLANES)
        v_t = jnp.concatenate([r[h].astype(bf16) for r in v_refs], axis=1)
        outs.append(_dot_nt(w_past[mine], v_t)
                    + jnp.dot(w_parts[pages][mine], vn[:, hsl(h)], preferred_element_type=f32))
    o_ref[...] = jnp.concatenate(outs, axis=1)[:steps]


def sb_sample(q, k_new, v_new, bias, cache_kt, cache_vt, page_table, layer):
    nseq, steps, width = q.shape
    heads = width // HEAD_DIM
    pages = page_table.shape[1]
    page = cache_kt.shape[-1]
    assert steps <= SUBLANES and cache_kt.shape[2:4] == (heads, HEAD_DIM)
    pad8 = lambda t: jnp.pad(t, ((0, 0), (0, SUBLANES - steps), (0, 0)))
    bias_rows = jnp.repeat(bias.astype(f32) * LOG2_E, SUBLANES).reshape(heads * SUBLANES, 1)
    m_later = _later_matrix(page, SB_SAMPLE_SPLIT_TERMS)
    row8 = pl.BlockSpec((None, SUBLANES, width), lambda b, pt: (b, 0, 0))
    page_spec = lambda j: pl.BlockSpec((None, None, heads, HEAD_DIM, page), lambda b, pt: (layer, pt[b, j], 0, 0, 0))
    grid_spec = pltpu.PrefetchScalarGridSpec(
        num_scalar_prefetch=1,
        grid=(nseq,),
        in_specs=[row8, row8, row8,
                  pl.BlockSpec(bias_rows.shape, lambda b, pt: (0, 0)),
                  pl.BlockSpec(m_later.shape, lambda b, pt: (0, 0))]
                 + [page_spec(j) for j in range(pages)] * 2,
        out_specs=pl.BlockSpec((None, steps, width), lambda b, pt: (b, 0, 0)),
    )
    return pl.pallas_call(
        functools.partial(_sb_sample_body, pages=pages, steps=steps),
        grid_spec=grid_spec,
        out_shape=jax.ShapeDtypeStruct((nseq, steps, width), f32),
        compiler_params=_params("arbitrary"),
        name="sb_sample",
    )(page_table, pad8(q), pad8(k_new), pad8(v_new), bias_rows, m_later,
      *([cache_kt] * pages), *([cache_vt] * pages))


def _out_ffn_body(x_ref, ysc_ref, yrw_ref, ysb_ref, wout_ref, wup_ref, wdown_ref, gains_ref,
                  o_ref, x1_scr, f_scr, acc_scr):
    j = pl.program_id(1)

    @pl.when(j == 0)
    def _():
        off, mix = 0, None
        for y_ref in (ysc_ref, yrw_ref, ysb_ref):
            n = y_ref.shape[-1]
            part = jnp.dot(y_ref[...].astype(bf16), wout_ref[off:off + n, :], preferred_element_type=f32)
            mix = part if mix is None else mix + part
            off += n
        x1 = x_ref[...] + _rms_rows(mix, gains_ref[0:1, :])
        x1_scr[...] = x1
        f_scr[...] = _rms_rows(x1, gains_ref[1:2, :]).astype(bf16)
        acc_scr[...] = jnp.zeros_like(acc_scr)

    hid = jnp.maximum(jnp.dot(f_scr[...], wup_ref[...], preferred_element_type=f32), 0.0)
    acc_scr[...] += jnp.dot((hid * hid).astype(bf16), wdown_ref[...], preferred_element_type=f32)

    @pl.when(j == pl.num_programs(1) - 1)
    def _():
        o_ref[...] = x1_scr[...] + _rms_rows(acc_scr[...], gains_ref[2:3, :])


def out_ffn(x, y_sc, y_rw, y_sb, w_out_bf, w_up_bf, w_down_bf, gains):
    rows, d = x.shape
    dff = w_up_bf.shape[1]
    tm = min(ROW_TILE, rows)
    tf = min(FF_TILE, dff)
    assert rows % tm == 0 and dff % tf == 0
    row_spec = lambda n: pl.BlockSpec((tm, n), lambda i, j: (i, 0))
    return pl.pallas_call(
        _out_ffn_body,
        grid=(rows // tm, dff // tf),
        in_specs=[row_spec(d), row_spec(y_sc.shape[1]), row_spec(y_rw.shape[1]), row_spec(y_sb.shape[1]),
                  pl.BlockSpec((d, d), lambda i, j: (0, 0)),
                  pl.BlockSpec((d, tf), lambda i, j: (0, j)),
                  pl.BlockSpec((tf, d), lambda i, j: (j, 0)),
                  pl.BlockSpec(gains.shape, lambda i, j: (0, 0))],
        out_specs=row_spec(d),
        out_shape=jax.ShapeDtypeStruct((rows, d), f32),
        scratch_shapes=[pltpu.VMEM((tm, d), f32), pltpu.VMEM((tm, d), bf16), pltpu.VMEM((tm, d), f32)],
        compiler_params=_params("arbitrary", "arbitrary"),
        name="out_ffn",
    )(x, y_sc, y_rw, y_sb, w_out_bf, w_up_bf, w_down_bf, gains)


def kernel(x_prompt, x_sample, state_conv, state_shift, state_rwkv, cache_k, cache_v, page_table, w_in, conv_w, mu_shift, w0, w2, a0, a2, g2, k_k, k_a, r_k, ln_w, ln_b, sb_bias, w_out, w_up, w_down, g_pre_mix, g_post_mix, g_pre_ffn, g_post_ffn):
    depth = w_in.shape[0]
    bp, tp, d = x_prompt.shape
    bs, ts, _ = x_sample.shape
    sc_w = conv_w.shape[-1]
    rw_w = w0.shape[-1]
    rw_proj = mu_shift.shape[-1]
    sb_heads = sb_bias.shape[-1]
    sb_w = sb_heads * HEAD_DIM
    rw_heads = rw_w // HEAD_DIM
    widths = (3 * sc_w, rw_proj, sb_w, sb_w, sb_w)
    assert sum(widths) == w_in.shape[-1]

    w_in_bf, w_out_bf, w_up_bf, w_down_bf = (w.astype(bf16) for w in (w_in, w_out, w_up, w_down))
    cache_kt = jnp.transpose(cache_k, (0, 1, 3, 4, 2))
    cache_vt = jnp.transpose(cache_v, (0, 1, 3, 4, 2))

    xp = x_prompt.reshape(bp * tp, d)
    xs = jnp.swapaxes(x_sample, 0, 1).reshape(ts * bs, d)
    conv0 = jnp.zeros((bp, 2, sc_w), f32)
    shift0 = jnp.zeros((bp, rw_proj), f32)
    rw0 = jnp.zeros((bp, rw_heads, HEAD_DIM, HEAD_DIM), f32)

    outs_p, outs_s = [], []
    for l in range(depth):
        vecs = _rwkv_vectors(mu_shift[l], w0[l], a0[l], k_k[l], k_a[l], r_k[l], ln_w[l], ln_b[l])
        gains = jnp.stack([g_post_mix[l], g_pre_ffn[l], g_post_ffn[l]])
        lora = (w2[l], a2[l], g2[l])

        p_sc, p_rw, q, k, v = in_proj(xp, g_pre_mix[l], w_in_bf[l], widths)
        y_sc, n_conv = sconv_prompt(p_sc, conv0, conv_w[l], bp)
        y_rw, n_shift, n_state = rwkv_prompt(p_rw, shift0, rw0, vecs, *lora, bp)
        y_sb = sb_prompt(q, k, v, sb_bias[l], bp)
        xp = out_ffn(xp, y_sc, y_rw, y_sb, w_out_bf[l], w_up_bf[l], w_down_bf[l], gains)
        outs_p.append((n_conv, n_shift.reshape(bp, rw_proj), n_state,
                       k.reshape(bp, tp, sb_heads, HEAD_DIM), v.reshape(bp, tp, sb_heads, HEAD_DIM)))

        p_sc, p_rw, q, k, v = in_proj(xs, g_pre_mix[l], w_in_bf[l], widths)
        y_sc, n_conv = sconv_sample(p_sc, jnp.swapaxes(state_conv[l], 0, 1), conv_w[l], ts)
        y_rw, n_shift, n_state = rwkv_sample(p_rw, state_shift[l], state_rwkv[l], vecs, *lora, ts)
        to_bm = lambda t: jnp.swapaxes(t.reshape(ts, bs, sb_w), 0, 1)
        y_sb = sb_sample(to_bm(q), to_bm(k), to_bm(v), sb_bias[l], cache_kt, cache_vt, page_table, l)
        y_sb = jnp.swapaxes(y_sb, 0, 1).reshape(ts * bs, sb_w)
        xs = out_ffn(xs, y_sc, y_rw, y_sb, w_out_bf[l], w_up_bf[l], w_down_bf[l], gains)
        outs_s.append((jnp.swapaxes(n_conv, 0, 1), n_shift, n_state,
                       to_bm(k).reshape(bs, ts, sb_heads, HEAD_DIM), to_bm(v).reshape(bs, ts, sb_heads, HEAD_DIM)))

    p_conv, p_shift, p_rwkv, p_k, p_v = (jnp.stack(t) for t in zip(*outs_p))
    s_conv, s_shift, s_rwkv, s_k, s_v = (jnp.stack(t) for t in zip(*outs_s))
    y_prompt = xp.reshape(bp, tp, d)
    y_sample = jnp.swapaxes(xs.reshape(ts, bs, d), 0, 1)
    return (y_prompt, y_sample, p_conv, p_shift, p_rwkv, p_k, p_v, s_conv, s_shift, s_rwkv, s_k, s_v)
```

```python
import functools

import jax
import jax.numpy as jnp
from jax import lax
from jax.experimental import pallas as pl
from jax.experimental.pallas import tpu as pltpu

f32 = jnp.float32
bf16 = jnp.bfloat16

HEAD_DIM = 64
NORM_EPS = 1e-6
RW_GN_EPS = HEAD_DIM * 1e-5
SB_SCALE = HEAD_DIM ** -0.5
LOG2_E = 1.4426950408889634
V7X_VMEM_LIMIT_BYTES = 56 * 1024 * 1024
V7X_MXU_LANES = 256
ROW_TILE = 512
FF_TILE = 1024
RW_CHUNK = 64
RW_BLOCK = 512
RW_PREP_GROUP = 8
SB_PROMPT_SPLIT_TERMS = 1
SB_SAMPLE_SPLIT_TERMS = 2
SB_TILE = 256
SAMPLE_GROUP = 8
SUBLANES = 8


def _params(*sem):
    return pltpu.CompilerParams(dimension_semantics=sem or None, vmem_limit_bytes=V7X_VMEM_LIMIT_BYTES)


def _bdot(a, b):
    return jnp.dot(a.astype(bf16), b.astype(bf16), preferred_element_type=f32)


def _dot_nt(a, b):
    return lax.dot_general(a.astype(bf16), b.astype(bf16), (((1,), (1,)), ((), ())), preferred_element_type=f32)


def _dot_tn(a, b):
    return lax.dot_general(a.astype(bf16), b.astype(bf16), (((0,), (0,)), ((), ())), preferred_element_type=f32)


def _split(x, terms):
    parts = []
    for _ in range(terms):
        p = x.astype(bf16)
        parts.append(p)
        x = x - p.astype(f32)
    return parts


def _dot_x01(x, m01, terms=2):
    return sum(jnp.dot(p, m01, preferred_element_type=f32) for p in _split(x, terms))


def _dot_01x(m01, x, terms=2):
    return sum(jnp.dot(m01, p, preferred_element_type=f32) for p in _split(x, terms))


def _rms_rows(x, g):
    return x * lax.rsqrt(jnp.mean(x * x, axis=-1, keepdims=True) + NORM_EPS) * g


def _softplus(x):
    return jnp.maximum(x, 0.0) + jnp.log(1.0 + jnp.exp(-jnp.abs(x)))


def _log2_one_minus_beta(n2):
    return jnp.minimum(n2, 0.0) - jnp.log2(1.0 + jnp.exp2(-jnp.abs(n2)))


def _sigmoid(x):
    return 1.0 / (1.0 + jnp.exp(-x))


def _lane_groups(width):
    return [(g, min(g + V7X_MXU_LANES, width)) for g in range(0, width, V7X_MXU_LANES)]


def _head_blocks(x, heads):
    c = x.shape[0]
    t = jnp.concatenate([x] * heads, axis=0)
    keep = (lax.broadcasted_iota(jnp.int32, t.shape, 0) // c) == (lax.broadcasted_iota(jnp.int32, t.shape, 1) // HEAD_DIM)
    return jnp.where(keep, t, jnp.zeros_like(t))


def _bd_dot(lhs, x):
    outs = []
    for g0, g1 in _lane_groups(x.shape[1]):
        xb = _head_blocks(x[:, g0:g1].astype(bf16), (g1 - g0) // HEAD_DIM)
        outs.append(jnp.dot(lhs[:, g0:g1].astype(bf16), xb, preferred_element_type=f32))
    return jnp.concatenate(outs, axis=1)


def _bd_dot_nt(lhs, x):
    outs = []
    for g0, g1 in _lane_groups(x.shape[1]):
        xb = _head_blocks(x[:, g0:g1].astype(bf16), (g1 - g0) // HEAD_DIM)
        outs.append(_dot_nt(lhs[:, g0:g1], xb))
    return jnp.concatenate(outs, axis=1)


def _in_proj_body(x_ref, g_ref, w_ref, *out_refs):
    h = _rms_rows(x_ref[...], g_ref[...]).astype(bf16)
    off = 0
    for o_ref in out_refs:
        n = o_ref.shape[-1]
        o_ref[...] = jnp.dot(h, w_ref[:, off:off + n], preferred_element_type=f32)
        off += n


def in_proj(x, g, w_bf, widths):
    rows, d = x.shape
    tm = min(ROW_TILE, rows)
    assert rows % tm == 0 and sum(widths) == w_bf.shape[1]
    return pl.pallas_call(
        _in_proj_body,
        grid=(rows // tm,),
        in_specs=[pl.BlockSpec((tm, d), lambda i: (i, 0)),
                  pl.BlockSpec((1, d), lambda i: (0, 0)),
                  pl.BlockSpec(w_bf.shape, lambda i: (0, 0))],
        out_specs=[pl.BlockSpec((tm, n), lambda i: (i, 0)) for n in widths],
        out_shape=[jax.ShapeDtypeStruct((rows, n), f32) for n in widths],
        compiler_params=_params("arbitrary"),
        name="in_proj",
    )(x, g.reshape(1, d), w_bf)


def _sconv_seq_body(p_ref, buf_ref, w_ref, y_ref, nb_ref):
    t, w = y_ref.shape
    gate = p_ref[:, 0:w]
    u = p_ref[:, w:2 * w] * p_ref[:, 2 * w:3 * w]
    row = lax.broadcasted_iota(jnp.int32, (t, w), 0)
    buf = buf_ref[...]
    um1 = jnp.where(row == 0, buf[1:2], pltpu.roll(u, 1, 0))
    um2 = jnp.where(row == 0, buf[0:1], jnp.where(row == 1, buf[1:2], pltpu.roll(u, 2, 0)))
    cw = w_ref[...]
    y_ref[...] = gate * (cw[0:1] * um2 + cw[1:2] * um1 + cw[2:3] * u)
    nb_ref[...] = u[t - 2:t]


def sconv_prompt(p_sc, conv_buf, conv_w, batch):
    rows, w3 = p_sc.shape
    t, w = rows // batch, w3 // 3
    assert conv_w.shape[0] == 3 and t >= 2
    return pl.pallas_call(
        _sconv_seq_body,
        grid=(batch,),
        in_specs=[pl.BlockSpec((t, w3), lambda b: (b, 0)),
                  pl.BlockSpec((None, 2, w), lambda b: (b, 0, 0)),
                  pl.BlockSpec((3, w), lambda b: (0, 0))],
        out_specs=[pl.BlockSpec((t, w), lambda b: (b, 0)),
                   pl.BlockSpec((None, 2, w), lambda b: (b, 0, 0))],
        out_shape=[jax.ShapeDtypeStruct((rows, w), f32), jax.ShapeDtypeStruct((batch, 2, w), f32)],
        compiler_params=_params("arbitrary"),
        name="sconv_prompt",
    )(p_sc, conv_buf, conv_w)


def _sconv_tm_body(p_ref, buf_ref, w_ref, y_ref, nb_ref, *, steps):
    rows, w = y_ref.shape
    nb = rows // steps
    gate = p_ref[:, 0:w]
    u = p_ref[:, w:2 * w] * p_ref[:, 2 * w:3 * w]
    pad = [buf_ref[0], buf_ref[1]] + [u[t * nb:(t + 1) * nb] for t in range(steps)]
    cw = w_ref[...]
    for t in range(steps):
        y_ref[t * nb:(t + 1) * nb, :] = gate[t * nb:(t + 1) * nb] * (
            cw[0:1] * pad[t] + cw[1:2] * pad[t + 1] + cw[2:3] * pad[t + 2])
    nb_ref[0] = pad[steps]
    nb_ref[1] = pad[steps + 1]


def sconv_sample(p_sc, conv_buf_tm, conv_w, steps):
    rows, w3 = p_sc.shape
    w = w3 // 3
    return pl.pallas_call(
        functools.partial(_sconv_tm_body, steps=steps),
        out_shape=[jax.ShapeDtypeStruct((rows, w), f32), jax.ShapeDtypeStruct(conv_buf_tm.shape, f32)],
        compiler_params=_params(),
        name="sconv_sample",
    )(p_sc, conv_buf_tm, conv_w)


def _rwkv_prep(p, prev, vec_ref, w2_ref, a2_ref, g2_ref, seg):
    rw = seg.shape[0]
    dl, al = w2_ref.shape[0], a2_ref.shape[0]
    mu = vec_ref[0:1, :]
    w0, a0, kkw, kaw = (vec_ref[i:i + 1, 0:rw] for i in (1, 2, 3, 4))
    xs = p + (prev - p) * mu
    r, k, v = xs[:, 0:rw], xs[:, rw:2 * rw], xs[:, 2 * rw:3 * rw]
    o = 3 * rw
    xw, xa, xg = xs[:, o:o + dl], xs[:, o + dl:o + dl + al], xs[:, o + dl + al:]
    log_w = -_softplus(-(w0 + _bdot(jnp.tanh(xw), w2_ref[...]))) - 0.5
    lw = -jnp.exp(log_w)
    a = _sigmoid(a0 + _bdot(xa, a2_ref[...]))
    g = _bdot(_sigmoid(xg), g2_ref[...])
    kk = k * kkw
    k = k * (1.0 + (a - 1.0) * kaw)
    kk = kk * jnp.minimum(lax.rsqrt(_dot_x01(kk * kk, seg, terms=1)), 1e12)
    return r, k, v, kk, a, lw, g


def _rwkv_finish(o, r, k, v, g, vec_ref, seg):
    rw = seg.shape[0]
    rk, lnw, lnb = (vec_ref[i:i + 1, 0:rw] for i in (5, 6, 7))
    inv_n = 1.0 / HEAD_DIM
    mean = _dot_x01(o, seg) * inv_n
    d = o - mean
    var = _dot_x01(d * d, seg, terms=1) * inv_n
    on = d * lax.rsqrt(var + RW_GN_EPS) * lnw + lnb
    bonus = _dot_x01(r * k * rk, seg, terms=1) * v
    return (on + bonus) * g


def _rwkv_vectors(mu, w0, a0, k_k, k_a, r_k, ln_w, ln_b):
    proj = mu.shape[-1]
    rows = [mu] + [jnp.pad(t.reshape(-1), (0, proj - t.size)) for t in (w0, a0, k_k, k_a, r_k, ln_w, ln_b)]
    return jnp.stack(rows).astype(f32)


def _seg_ones(width):
    i = jnp.arange(width) // HEAD_DIM
    return (i[:, None] == i[None, :]).astype(bf16)


def _rwkv_prompt_body(p_ref, shift_ref, s0_ref, vec_ref, w2_ref, a2_ref, g2_ref, seg_ref, tri_ref,
                      y_ref, nshift_ref, st_ref,
                      s_scr, last_scr, r_scr, k_scr, v_scr, kk_scr, a_scr, lw_scr, o_scr,
                      ar_scr, tm_scr, lv_scr, arb_scr, bkg_scr, g_scr):
    tb = pl.program_id(1)
    rows, rw = y_ref.shape
    heads = rw // HEAD_DIM
    c = RW_CHUNK
    nchunk = rows // c
    hsl = lambda h: slice(h * HEAD_DIM, (h + 1) * HEAD_DIM)

    @pl.when(tb == 0)
    def _():
        s_scr[...] = jnp.zeros_like(s_scr)
        for h in range(heads):
            s_scr[hsl(h), hsl(h)] = s0_ref[h]
        last_scr[...] = shift_ref[...]

    p = p_ref[...]
    row = lax.broadcasted_iota(jnp.int32, p.shape, 0)
    prev = jnp.where(row == 0, last_scr[...], pltpu.roll(p, 1, 0))
    last_scr[...] = p[rows - 1:rows]
    nshift_ref[...] = p[rows - 1:rows]
    seg = seg_ref[...]
    r, k, v, kk, a, lw, g = _rwkv_prep(p, prev, vec_ref, w2_ref, a2_ref, g2_ref, seg)
    r_scr[...] = r
    k_scr[...] = k
    v_scr[...] = v
    kk_scr[...] = kk
    a_scr[...] = a
    lw_scr[...] = lw

    group = RW_PREP_GROUP if nchunk % RW_PREP_GROUP == 0 else 1

    def prepare(n, carry):
        ri = lax.broadcasted_iota(jnp.int32, (2 * c, rw), 0)
        cj = lax.broadcasted_iota(jnp.int32, (2 * c, rw), 1) % HEAD_DIM
        tri_mask = cj < jnp.where(ri < c, ri, ri - c + 1)
        eye = (lax.broadcasted_iota(jnp.int32, (c, rw), 0)
               == lax.broadcasted_iota(jnp.int32, (c, rw), 1) % HEAD_DIM).astype(f32)
        gs = range(group)
        sl_t = [pl.ds(pl.multiple_of((n * group + i) * c, c), c) for i in gs]
        sl_2t = [pl.ds(pl.multiple_of((n * group + i) * 2 * c, 2 * c), 2 * c) for i in gs]
        ld = lambda scr: [scr[sl_t[i], :] for i in gs]
        rc, kc, vc, kkc, ac, lwc = (ld(s) for s in (r_scr, k_scr, v_scr, kk_scr, a_scr, lw_scr))
        cs = [_dot_01x(tri_ref[...], lwc[i], terms=3) for i in gs]
        tot = [cs[i][c - 1:c, :] for i in gs]
        e_neg = [jnp.exp(-cs[i]) for i in gs]
        e_rem = [jnp.exp(tot[i] - cs[i]) for i in gs]
        kka = [kkc[i] * ac[i] for i in gs]
        ar = [jnp.concatenate([-kkc[i] * jnp.exp(cs[i] - lwc[i]), rc[i] * jnp.exp(cs[i])], axis=0).astype(bf16)
              for i in gs]
        b_t = [(kka[i] * e_neg[i]).astype(bf16) for i in gs]
        k_t = [(kc[i] * e_neg[i]).astype(bf16) for i in gs]
        m_b = [jnp.where(tri_mask, _bd_dot_nt(ar[i], b_t[i]), 0.0) for i in gs]
        m_k = [jnp.where(tri_mask, _bd_dot_nt(ar[i], k_t[i]), 0.0) for i in gs]
        pw = [m_b[i][:c] for i in gs]
        tm = [eye + pw[i] for i in gs]
        span = 1
        while 2 * span < c:
            pw = [_bd_dot(pw[i], pw[i]) for i in gs]
            tm = [tm[i] + _bd_dot(tm[i], pw[i]) for i in gs]
            span *= 2
        lv = [_bd_dot(m_k[i], vc[i]) for i in gs]
        for i in gs:
            ar_scr[sl_2t[i], :] = ar[i]
            tm_scr[sl_t[i], :] = tm[i].astype(bf16)
            lv_scr[sl_2t[i], :] = lv[i]
            arb_scr[sl_t[i], :] = m_b[i][c:].astype(bf16)
            bkg_scr[sl_2t[i], :] = jnp.concatenate([kka[i] * e_rem[i], kc[i] * e_rem[i]], axis=0).astype(bf16)
            g_scr[pl.ds(n * group + i, 1), :] = jnp.exp(tot[i])
        return carry

    lax.fori_loop(0, nchunk // group, prepare, 0)

    def advance(n, carry):
        sl_t = pl.ds(pl.multiple_of(n * c, c), c)
        sl_2t = pl.ds(pl.multiple_of(n * 2 * c, 2 * c), 2 * c)
        ar = ar_scr[sl_2t, :]
        x = jnp.concatenate([_dot_nt(ar[:, g0:g1], s_scr[g0:g1, g0:g1]) for g0, g1 in _lane_groups(rw)], axis=1)
        x = x + lv_scr[sl_2t, :]
        u = _bd_dot(tm_scr[sl_t, :], x[:c])
        o_scr[sl_t, :] = x[c:] + _bd_dot(arb_scr[sl_t, :], u)
        uv = jnp.concatenate([u.astype(bf16), v_scr[sl_t, :].astype(bf16)], axis=0)
        bkg = bkg_scr[sl_2t, :]
        gdec = g_scr[pl.ds(n, 1), :]
        for g0, g1 in _lane_groups(rw):
            upd = _dot_tn(uv[:, g0:g1], bkg[:, g0:g1])
            same_head = (lax.broadcasted_iota(jnp.int32, upd.shape, 0) // HEAD_DIM
                         == lax.broadcasted_iota(jnp.int32, upd.shape, 1) // HEAD_DIM)
            s_scr[g0:g1, g0:g1] = s_scr[g0:g1, g0:g1] * gdec[:, g0:g1] + jnp.where(same_head, upd, 0.0)
        return carry

    lax.fori_loop(0, nchunk, advance, 0)
    y_ref[...] = _rwkv_finish(o_scr[...], r, k, v, g, vec_ref, seg)

    @pl.when(tb == pl.num_programs(1) - 1)
    def _():
        for h in range(heads):
            st_ref[h] = s_scr[hsl(h), hsl(h)]


def rwkv_prompt(p_rw, shift_buf, state0, vecs, w2, a2, g2, batch):
    rows, proj = p_rw.shape
    t = rows // batch
    rw = w2.shape[1]
    heads = rw // HEAD_DIM
    blk = min(RW_BLOCK, t)
    assert t % blk == 0 and blk % RW_CHUNK == 0 and RW_CHUNK == HEAD_DIM
    nblk = t // blk
    nchunk = blk // RW_CHUNK
    tri = jnp.tril(jnp.ones((RW_CHUNK, RW_CHUNK), bf16))
    const = lambda shape: pl.BlockSpec(shape, lambda b, i: (0,) * len(shape))
    return pl.pallas_call(
        _rwkv_prompt_body,
        grid=(batch, nblk),
        in_specs=[pl.BlockSpec((blk, proj), lambda b, i: (b * nblk + i, 0)),
                  pl.BlockSpec((None, 1, proj), lambda b, i: (b, 0, 0)),
                  pl.BlockSpec((None, heads, HEAD_DIM, HEAD_DIM), lambda b, i: (b, 0, 0, 0)),
                  const(vecs.shape), const(w2.shape), const(a2.shape), const(g2.shape),
                  const((rw, rw)), const(tri.shape)],
        out_specs=[pl.BlockSpec((blk, rw), lambda b, i: (b * nblk + i, 0)),
                   pl.BlockSpec((None, 1, proj), lambda b, i: (b, 0, 0)),
                   pl.BlockSpec((None, heads, HEAD_DIM, HEAD_DIM), lambda b, i: (b, 0, 0, 0))],
        out_shape=[jax.ShapeDtypeStruct((rows, rw), f32),
                   jax.ShapeDtypeStruct((batch, 1, proj), f32),
                   jax.ShapeDtypeStruct(state0.shape, f32)],
        scratch_shapes=[pltpu.VMEM((rw, rw), f32), pltpu.VMEM((1, proj), f32)]
                       + [pltpu.VMEM((blk, rw), f32)] * 7
                       + [pltpu.VMEM((2 * blk, rw), bf16), pltpu.VMEM((blk, rw), bf16), pltpu.VMEM((2 * blk, rw), f32),
                          pltpu.VMEM((blk, rw), bf16), pltpu.VMEM((2 * blk, rw), bf16), pltpu.VMEM((nchunk, rw), f32)],
        compiler_params=_params("arbitrary", "arbitrary"),
        name="rwkv_prompt",
    )(p_rw, shift_buf.reshape(batch, 1, proj), state0, vecs, w2, a2, g2, _seg_ones(rw), tri)


def _rwkv_sample_body(p_ref, shift_ref, s0_ref, vec_ref, w2_ref, a2_ref, g2_ref, seg_ref,
                      y_ref, nshift_ref, st_ref,
                      r_scr, k_scr, v_scr, kk_scr, kka_scr, dec_scr, o_scr):
    steps, nb, proj = p_ref.shape
    rw = y_ref.shape[-1]
    heads = rw // HEAD_DIM
    p = p_ref[...].reshape(steps * nb, proj)
    prev = jnp.concatenate([shift_ref[...], p[:(steps - 1) * nb]], axis=0)
    nshift_ref[...] = p[(steps - 1) * nb:]
    seg = seg_ref[...]
    r, k, v, kk, a, lw, g = _rwkv_prep(p, prev, vec_ref, w2_ref, a2_ref, g2_ref, seg)
    r_scr[...] = r
    k_scr[...] = k
    v_scr[...] = v
    kk_scr[...] = kk
    kka_scr[...] = kk * a
    dec_scr[...] = jnp.exp(lw)
    n = HEAD_DIM
    hs = range(heads)
    sl = [slice(h * n, (h + 1) * n) for h in hs]

    def one_sequence(i, carry):
        eye = (lax.broadcasted_iota(jnp.int32, (n, n), 0) == lax.broadcasted_iota(jnp.int32, (n, n), 1)).astype(f32)
        rep = lambda row: jnp.broadcast_to(row, (SUBLANES, n))
        s = [s0_ref[i, h] for h in hs]
        for t in range(steps):
            at = pl.ds(t * nb + i, 1)
            r_t, k_t, v_t, kk_t, kka_t, dec_t = (scr[at, :] for scr in (r_scr, k_scr, v_scr, kk_scr, kka_scr, dec_scr))
            sa = [-_dot_nt(s[h], rep(kk_t[:, sl[h]]))[:, 0:1] for h in hs]
            v_col = [_dot_nt(eye, rep(v_t[:, sl[h]]))[:, 0:1] for h in hs]
            s = [s[h] * dec_t[:, sl[h]] + sa[h] * kka_t[:, sl[h]] + v_col[h] * k_t[:, sl[h]] for h in hs]
            o_scr[at, :] = jnp.concatenate([_dot_nt(rep(r_t[:, sl[h]]), s[h])[0:1] for h in hs], axis=1)
        for h in hs:
            st_ref[i, h] = s[h]
        return carry

    lax.fori_loop(0, nb, one_sequence, 0)
    y_ref[...] = _rwkv_finish(o_scr[...], r, k, v, g, vec_ref, seg).reshape(steps, nb, rw)


def rwkv_sample(p_rw, shift_buf, state0, vecs, w2, a2, g2, steps):
    rows, proj = p_rw.shape
    nseq = rows // steps
    rw = w2.shape[1]
    heads = rw // HEAD_DIM
    nb = SAMPLE_GROUP
    assert nseq % nb == 0
    const = lambda shape: pl.BlockSpec(shape, lambda j: (0,) * len(shape))
    y, nshift, st = pl.pallas_call(
        _rwkv_sample_body,
        grid=(nseq // nb,),
        in_specs=[pl.BlockSpec((steps, nb, proj), lambda j: (0, j, 0)),
                  pl.BlockSpec((nb, proj), lambda j: (j, 0)),
                  pl.BlockSpec((nb, heads, HEAD_DIM, HEAD_DIM), lambda j: (j, 0, 0, 0)),
                  const(vecs.shape), const(w2.shape), const(a2.shape), const(g2.shape), const((rw, rw))],
        out_specs=[pl.BlockSpec((steps, nb, rw), lambda j: (0, j, 0)),
                   pl.BlockSpec((nb, proj), lambda j: (j, 0)),
                   pl.BlockSpec((nb, heads, HEAD_DIM, HEAD_DIM), lambda j: (j, 0, 0, 0))],
        out_shape=[jax.ShapeDtypeStruct((steps, nseq, rw), f32),
                   jax.ShapeDtypeStruct((nseq, proj), f32),
                   jax.ShapeDtypeStruct(state0.shape, f32)],
        scratch_shapes=[pltpu.VMEM((steps * nb, rw), f32)] * 7,
        compiler_params=_params("arbitrary"),
        name="rwkv_sample",
    )(p_rw.reshape(steps, nseq, proj), shift_buf, state0, vecs, w2, a2, g2, _seg_ones(rw))
    return y.reshape(rows, rw), nshift, st


def _later_matrix(n, terms):
    idx = jnp.arange(n)
    m_later = (idx[:, None] > idx[None, :]).astype(bf16)
    return jnp.concatenate([m_later] * terms, axis=0)


def _sb_prompt_body(bias_ref, q_ref, k_ref, v_ref, m_ref, o_ref):
    i = pl.program_id(1)
    tq, width = q_ref.shape
    heads = width // HEAD_DIM
    tk = tq
    terms = m_ref.shape[0] // tk
    m_later = m_ref[...]
    hs = range(heads)
    sl = [slice(h * HEAD_DIM, (h + 1) * HEAD_DIM) for h in hs]
    qn = [(q_ref[:, sl[h]] * (-SB_SCALE * LOG2_E)).astype(bf16) for h in hs]

    def tile(kj, carries, diagonal):
        at = pl.ds(pl.multiple_of(kj * tk, tk), tk)
        if diagonal:
            below = lax.broadcasted_iota(jnp.int32, (tq, tk), 1) < lax.broadcasted_iota(jnp.int32, (tq, tk), 0)
        n2 = [_dot_nt(qn[h], k_ref[at, sl[h]]) - bias_ref[h] for h in hs]
        l1m = [_log2_one_minus_beta(n2[h]) for h in hs]
        log_beta = [l1m[h] - n2[h] for h in hs]
        if diagonal:
            l1m = [jnp.where(below, l1m[h], 0.0) for h in hs]
        parts = [jnp.concatenate(_split(l1m[h], terms), axis=1) for h in hs]
        suffix = [jnp.dot(parts[h], m_later, preferred_element_type=f32) for h in hs]
        w = [jnp.exp2(log_beta[h] + suffix[h] + carries[h]) for h in hs]
        if diagonal:
            w = [jnp.where(below, w[h], 0.0) for h in hs]
        out = [_bdot(w[h], v_ref[at, sl[h]]) for h in hs]
        for h in hs:
            if diagonal:
                o_ref[:, sl[h]] = out[h]
            else:
                o_ref[:, sl[h]] += out[h]
        return tuple(carries[h] + jnp.sum(l1m[h], axis=1, keepdims=True) for h in hs)

    carries = tile(i, tuple(jnp.zeros((tq, 1), f32) for _ in hs), True)
    lax.fori_loop(0, i, lambda t, c: tile(i - 1 - t, c, False), carries)


def sb_prompt(q, k, v, bias, batch):
    rows, width = q.shape
    t = rows // batch
    tq = min(SB_TILE, t)
    assert t % tq == 0
    nq = t // tq
    m_later = _later_matrix(tq, SB_PROMPT_SPLIT_TERMS)
    return pl.pallas_call(
        _sb_prompt_body,
        grid=(batch, nq),
        in_specs=[pl.BlockSpec(memory_space=pltpu.SMEM),
                  pl.BlockSpec((tq, width), lambda b, i: (b * nq + i, 0)),
                  pl.BlockSpec((t, width), lambda b, i: (b, 0)),
                  pl.BlockSpec((t, width), lambda b, i: (b, 0)),
                  pl.BlockSpec(m_later.shape, lambda b, i: (0, 0))],
        out_specs=pl.BlockSpec((tq, width), lambda b, i: (b * nq + i, 0)),
        out_shape=jax.ShapeDtypeStruct((rows, width), f32),
        compiler_params=_params("arbitrary", "arbitrary"),
        name="sb_prompt",
    )(bias.astype(f32) * LOG2_E, q, k, v, m_later)


def _sb_sample_body(pt_ref, q_ref, kn_ref, vn_ref, bias_ref, m_ref, *rest, pages, steps):
    k_refs, v_refs, o_ref = rest[:pages], rest[pages:2 * pages], rest[2 * pages]
    del pt_ref
    width = q_ref.shape[-1]
    heads = width // HEAD_DIM
    page = m_ref.shape[1]
    terms = m_ref.shape[0] // page
    rows = heads * SUBLANES
    hsl = lambda h: slice(h * HEAD_DIM, (h + 1) * HEAD_DIM)
    qn = (q_ref[...] * (-SB_SCALE * LOG2_E)).astype(bf16)
    fill = jnp.zeros((page - SUBLANES, width), f32)
    kn = jnp.concatenate([kn_ref[...], fill], axis=0).astype(bf16)
    vn = jnp.concatenate([vn_ref[...], fill], axis=0).astype(bf16)

    n2 = jnp.concatenate(
        [jnp.concatenate([jnp.dot(qn[:, hsl(h)], jnp.concatenate([r[h].astype(bf16) for r in k_refs], axis=1),
                                  preferred_element_type=f32),
                          _dot_nt(qn[:, hsl(h)], kn[:, hsl(h)])], axis=1)
         for h in range(heads)], axis=0) - bias_ref[...]
    l1m = _log2_one_minus_beta(n2)
    log_beta = l1m - n2
    seen = (lax.broadcasted_iota(jnp.int32, (rows, page), 1)
            < lax.broadcasted_iota(jnp.int32, (rows, page), 0) % SUBLANES)

    m_later = m_ref[...]
    carry = jnp.zeros((rows, 1), f32)
    w_parts = [None] * (pages + 1)
    for j in range(pages, -1, -1):
        at = slice(j * page, (j + 1) * page)
        lj = l1m[:, at]
        if j == pages:
            lj = jnp.where(seen, lj, 0.0)
        suffix = jnp.dot(jnp.concatenate(_split(lj, terms), axis=1), m_later, preferred_element_type=f32)
        w = jnp.exp2(log_beta[:, at] + suffix + carry)
        if j == pages:
            w = jnp.where(seen, w, 0.0)
        w_parts[j] = w.astype(bf16)
        carry = carry + jnp.sum(lj, axis=1, keepdims=True)
    w_past = jnp.concatenate(w_parts[:pages], axis=1)
    outs = []
    for h in range(heads):
        mine = slice(h * SUBLANES, (h + 1) * SUBLANES)
        v_t = jnp.concatenate([r[h].astype(bf16) for r in v_refs], axis=1)
        outs.append(_dot_nt(w_past[mine], v_t)
                    + jnp.dot(w_parts[pages][mine], vn[:, hsl(h)], preferred_element_type=f32))
    o_ref[...] = jnp.concatenate(outs, axis=1)[:steps]


def sb_sample(q, k_new, v_new, bias, cache_kt, cache_vt, page_table, layer):
    nseq, steps, width = q.shape
    heads = width // HEAD_DIM
    pages = page_table.shape[1]
    page = cache_kt.shape[-1]
    assert steps <= SUBLANES and cache_kt.shape[2:4] == (heads, HEAD_DIM)
    pad8 = lambda t: jnp.pad(t, ((0, 0), (0, SUBLANES - steps), (0, 0)))
    bias_rows = jnp.repeat(bias.astype(f32) * LOG2_E, SUBLANES).reshape(heads * SUBLANES, 1)
    m_later = _later_matrix(page, SB_SAMPLE_SPLIT_TERMS)
    row8 = pl.BlockSpec((None, SUBLANES, width), lambda b, pt: (b, 0, 0))
    page_spec = lambda j: pl.BlockSpec((None, None, heads, HEAD_DIM, page), lambda b, pt: (layer, pt[b, j], 0, 0, 0))
    grid_spec = pltpu.PrefetchScalarGridSpec(
        num_scalar_prefetch=1,
        grid=(nseq,),
        in_specs=[row8, row8, row8,
                  pl.BlockSpec(bias_rows.shape, lambda b, pt: (0, 0)),
                  pl.BlockSpec(m_later.shape, lambda b, pt: (0, 0))]
                 + [page_spec(j) for j in range(pages)] * 2,
        out_specs=pl.BlockSpec((None, steps, width), lambda b, pt: (b, 0, 0)),
    )
    return pl.pallas_call(
        functools.partial(_sb_sample_body, pages=pages, steps=steps),
        grid_spec=grid_spec,
        out_shape=jax.ShapeDtypeStruct((nseq, steps, width), f32),
        compiler_params=_params("arbitrary"),
        name="sb_sample",
    )(page_table, pad8(q), pad8(k_new), pad8(v_new), bias_rows, m_later,
      *([cache_kt] * pages), *([cache_vt] * pages))


def _out_ffn_body(x_ref, ysc_ref, yrw_ref, ysb_ref, wout_ref, wup_ref, wdown_ref, gains_ref,
                  o_ref, x1_scr, f_scr, acc_scr):
    j = pl.program_id(1)

    @pl.when(j == 0)
    def _():
        off, mix = 0, None
        for y_ref in (ysc_ref, yrw_ref, ysb_ref):
            n = y_ref.shape[-1]
            part = jnp.dot(y_ref[...].astype(bf16), wout_ref[off:off + n, :], preferred_element_type=f32)
            mix = part if mix is None else mix + part
            off += n
        x1 = x_ref[...] + _rms_rows(mix, gains_ref[0:1, :])
        x1_scr[...] = x1
        f_scr[...] = _rms_rows(x1, gains_ref[1:2, :]).astype(bf16)
        acc_scr[...] = jnp.zeros_like(acc_scr)

    hid = jnp.maximum(jnp.dot(f_scr[...], wup_ref[...], preferred_element_type=f32), 0.0)
    acc_scr[...] += jnp.dot((hid * hid).astype(bf16), wdown_ref[...], preferred_element_type=f32)

    @pl.when(j == pl.num_programs(1) - 1)
    def _():
        o_ref[...] = x1_scr[...] + _rms_rows(acc_scr[...], gains_ref[2:3, :])


def out_ffn(x, y_sc, y_rw, y_sb, w_out_bf, w_up_bf, w_down_bf, gains):
    rows, d = x.shape
    dff = w_up_bf.shape[1]
    tm = min(ROW_TILE, rows)
    tf = min(FF_TILE, dff)
    assert rows % tm == 0 and dff % tf == 0
    row_spec = lambda n: pl.BlockSpec((tm, n), lambda i, j: (i, 0))
    return pl.pallas_call(
        _out_ffn_body,
        grid=(rows // tm, dff // tf),
        in_specs=[row_spec(d), row_spec(y_sc.shape[1]), row_spec(y_rw.shape[1]), row_spec(y_sb.shape[1]),
                  pl.BlockSpec((d, d), lambda i, j: (0, 0)),
                  pl.BlockSpec((d, tf), lambda i, j: (0, j)),
                  pl.BlockSpec((tf, d), lambda i, j: (j, 0)),
                  pl.BlockSpec(gains.shape, lambda i, j: (0, 0))],
        out_specs=row_spec(d),
        out_shape=jax.ShapeDtypeStruct((rows, d), f32),
        scratch_shapes=[pltpu.VMEM((tm, d), f32), pltpu.VMEM((tm, d), bf16), pltpu.VMEM((tm, d), f32)],
        compiler_params=_params("arbitrary", "arbitrary"),
        name="out_ffn",
    )(x, y_sc, y_rw, y_sb, w_out_bf, w_up_bf, w_down_bf, gains)


def kernel(x_prompt, x_sample, state_conv, state_shift, state_rwkv, cache_k, cache_v, page_table, w_in, conv_w, mu_shift, w0, w2, a0, a2, g2, k_k, k_a, r_k, ln_w, ln_b, sb_bias, w_out, w_up, w_down, g_pre_mix, g_post_mix, g_pre_ffn, g_post_ffn):
    depth = w_in.shape[0]
    bp, tp, d = x_prompt.shape
    bs, ts, _ = x_sample.shape
    sc_w = conv_w.shape[-1]
    rw_w = w0.shape[-1]
    rw_proj = mu_shift.shape[-1]
    sb_heads = sb_bias.shape[-1]
    sb_w = sb_heads * HEAD_DIM
    rw_heads = rw_w // HEAD_DIM
    widths = (3 * sc_w, rw_proj, sb_w, sb_w, sb_w)
    assert sum(widths) == w_in.shape[-1]

    w_in_bf, w_out_bf, w_up_bf, w_down_bf = (w.astype(bf16) for w in (w_in, w_out, w_up, w_down))
    cache_kt = jnp.transpose(cache_k, (0, 1, 3, 4, 2))
    cache_vt = jnp.transpose(cache_v, (0, 1, 3, 4, 2))

    xp = x_prompt.reshape(bp * tp, d)
    xs = jnp.swapaxes(x_sample, 0, 1).reshape(ts * bs, d)
    conv0 = jnp.zeros((bp, 2, sc_w), f32)
    shift0 = jnp.zeros((bp, rw_proj), f32)
    rw0 = jnp.zeros((bp, rw_heads, HEAD_DIM, HEAD_DIM), f32)

    outs_p, outs_s = [], []
    for l in range(depth):
        vecs = _rwkv_vectors(mu_shift[l], w0[l], a0[l], k_k[l], k_a[l], r_k[l], ln_w[l], ln_b[l])
        gains = jnp.stack([g_post_mix[l], g_pre_ffn[l], g_post_ffn[l]])
        lora = (w2[l], a2[l], g2[l])

        p_sc, p_rw, q, k, v = in_proj(xp, g_pre_mix[l], w_in_bf[l], widths)
        y_sc, n_conv = sconv_prompt(p_sc, conv0, conv_w[l], bp)
        y_rw, n_shift, n_state = rwkv_prompt(p_rw, shift0, rw0, vecs, *lora, bp)
        y_sb = sb_prompt(q, k, v, sb_bias[l], bp)
        xp = out_ffn(xp, y_sc, y_rw, y_sb, w_out_bf[l], w_up_bf[l], w_down_bf[l], gains)
        outs_p.append((n_conv, n_shift.reshape(bp, rw_proj), n_state,
                       k.reshape(bp, tp, sb_heads, HEAD_DIM), v.reshape(bp, tp, sb_heads, HEAD_DIM)))

        p_sc, p_rw, q, k, v = in_proj(xs, g_pre_mix[l], w_in_bf[l], widths)
        y_sc, n_conv = sconv_sample(p_sc, jnp.swapaxes(state_conv[l], 0, 1), conv_w[l], ts)
        y_rw, n_shift, n_state = rwkv_sample(p_rw, state_shift[l], state_rwkv[l], vecs, *lora, ts)
        to_bm = lambda t: jnp.swapaxes(t.reshape(ts, bs, sb_w), 0, 1)
        y_sb = sb_sample(to_bm(q), to_bm(k), to_bm(v), sb_bias[l], cache_kt, cache_vt, page_table, l)
        y_sb = jnp.swapaxes(y_sb, 0, 1).reshape(ts * bs, sb_w)
        xs = out_ffn(xs, y_sc, y_rw, y_sb, w_out_bf[l], w_up_bf[l], w_down_bf[l], gains)
        outs_s.append((jnp.swapaxes(n_conv, 0, 1), n_shift, n_state,
                       to_bm(k).reshape(bs, ts, sb_heads, HEAD_DIM), to_bm(v).reshape(bs, ts, sb_heads, HEAD_DIM)))

    p_conv, p_shift, p_rwkv, p_k, p_v = (jnp.stack(t) for t in zip(*outs_p))
    s_conv, s_shift, s_rwkv, s_k, s_v = (jnp.stack(t) for t in zip(*outs_s))
    y_prompt = xp.reshape(bp, tp, d)
    y_sample = jnp.swapaxes(xs.reshape(ts, bs, d), 0, 1)
    return (y_prompt, y_sample, p_conv, p_shift, p_rwkv, p_k, p_v, s_conv, s_shift, s_rwkv, s_k, s_v)
```
